```python
import jax, jax.numpy as jnp
from jax import lax
import numpy as np

D_MODEL = 1024
BATCH = 4
SEQ = 4096
DEPTH = 4

D_MIX = D_MODEL
ATT_HEAD_DIM = 64
ATT_WIDTH = D_MIX // 2
ATT_HEADS = ATT_WIDTH // ATT_HEAD_DIM
DILATED_GROUPS = ((128, 1), (512, 4), (2048, 16))
ROPE_THETA = 500000.0
ROPE_DIM = ATT_HEAD_DIM // 4
MLSTM_WIDTH = D_MIX - ATT_WIDTH
MLSTM_HEADS = 4
MLSTM_HEAD_DIM = MLSTM_WIDTH // MLSTM_HEADS
MLSTM_CHUNK = 64
CONV_WIDTH = 5
N_GATES = 4
IN_WIDTH = 3 * ATT_WIDTH + 4 * MLSTM_WIDTH + N_GATES * MLSTM_HEADS
D_FF = ((8 * D_MODEL // 3 + 127) // 128) * 128
N_NORMS = 6
NORM_EPS = 1e-6
NEG_BIG = -1e30

kernel_name = "hybrid_dilated_attn_mlstm_macaron"


def rms_norm(x, gain):
    xf = x.astype(jnp.float32)
    y = xf * lax.rsqrt(jnp.mean(xf * xf, axis=-1, keepdims=True) + NORM_EPS)
    return (y * gain).astype(x.dtype)


def swiglu(h, w_in, w_out):
    gu = h @ w_in
    g, u = jnp.split(gu, 2, axis=-1)
    return (jax.nn.silu(g) * u) @ w_out


def split_heads(t, n_heads):
    b, s, w = t.shape
    return t.reshape(b, s, n_heads, w // n_heads).transpose(0, 2, 1, 3)


def merge_heads(t):
    b, h, s, dh = t.shape
    return t.transpose(0, 2, 1, 3).reshape(b, s, h * dh)


def partial_rotary(t, positions):
    half = ROPE_DIM // 2
    inv_freq = ROPE_THETA ** (-jnp.arange(half, dtype=jnp.float32) / half)
    ang = positions.astype(jnp.float32)[:, None] * inv_freq[None, :]
    cos, sin = jnp.cos(ang), jnp.sin(ang)
    t1 = t[..., :half].astype(jnp.float32)
    t2 = t[..., half:ROPE_DIM].astype(jnp.float32)
    rot = jnp.concatenate([t1 * cos - t2 * sin, t2 * cos + t1 * sin], axis=-1).astype(t.dtype)
    return jnp.concatenate([rot, t[..., ROPE_DIM:]], axis=-1)


def dilated_window_attention(q, k, v, window, dilation):
    b, h, s, dh = q.shape
    n_side = (window // 2) // dilation
    blk = n_side
    unit = dilation * blk
    s_pad = -(-s // unit) * unit
    n_blk = s_pad // unit
    sub_len = s_pad // dilation

    def to_blocks(t):
        t = jnp.pad(t, ((0, 0), (0, 0), (0, s_pad - s), (0, 0)))
        t = t.reshape(b, h, sub_len, dilation, dh).transpose(0, 1, 3, 2, 4)
        return t.reshape(b, h, dilation, n_blk, blk, dh)

    def neighbours(t):
        tp = jnp.pad(t, ((0, 0), (0, 0), (0, 0), (1, 1), (0, 0), (0, 0)))
        return jnp.concatenate([tp[:, :, :, :-2], tp[:, :, :, 1:-1], tp[:, :, :, 2:]], axis=4)

    qb = to_blocks(q)
    kn = neighbours(to_blocks(k))
    vn = neighbours(to_blocks(v))

    r = jnp.arange(dilation)[:, None, None, None]
    bi = jnp.arange(n_blk)[None, :, None, None]
    qi = jnp.arange(blk)[None, None, :, None]
    kj = jnp.arange(3 * blk)[None, None, None, :]
    m_q = bi * blk + qi
    m_k = bi * blk - blk + kj
    valid = (m_k >= 0) & (m_k * dilation + r < s) & (jnp.abs(m_q - m_k) <= n_side)

    scores = jnp.einsum('bhrnid,bhrnjd->bhrnij', qb, kn).astype(jnp.float32)
    scores = jnp.where(valid, scores, NEG_BIG)
    lse = jax.nn.logsumexp(scores, axis=-1)
    p = jnp.exp(scores - lse[..., None])
    out = jnp.einsum('bhrnij,bhrnjd->bhrnid', p, vn.astype(jnp.float32))

    def from_blocks(t):
        rest = t.shape[5:]
        t = t.reshape((b, h, dilation, sub_len) + rest)
        t = jnp.moveaxis(t, 2, 3).reshape((b, h, s_pad) + rest)
        return t[:, :, :s]

    return from_blocks(out), from_blocks(lse)


def dilated_attention_mixture(q, k, v):
    outs, lses = [], []
    for window, dilation in DILATED_GROUPS:
        o, l = dilated_window_attention(q, k, v, window, dilation)
        outs.append(o)
        lses.append(l)
    weights = jax.nn.softmax(jnp.stack(lses, axis=0), axis=0)
    return jnp.sum(weights[..., None] * jnp.stack(outs, axis=0), axis=0)


def mlstm_direction(q, k, v, i_pre, f_pre):
    b, h, s, dh = q.shape
    n_chunks = s // MLSTM_CHUNK
    log_f = jax.nn.log_sigmoid(f_pre)
    tril = jnp.tril(jnp.ones((MLSTM_CHUNK, MLSTM_CHUNK), dtype=bool))

    def chunks(t):
        return jnp.moveaxis(t.reshape((b, h, n_chunks, MLSTM_CHUNK) + t.shape[3:]), 2, 0)

    def step(carry, inp):
        c_mat, n_vec, m_prev = carry
        qc, kc, vc, ic, lfc = inp
        cum = jnp.cumsum(lfc, axis=-1)
        decay = cum[..., :, None] - cum[..., None, :] + ic[..., None, :]
        decay = jnp.where(tril, decay, -jnp.inf)
        inter = cum + m_prev[..., None]
        m_t = jnp.maximum(inter, jnp.max(decay, axis=-1))
        w_intra = jnp.exp(decay - m_t[..., None])
        w_inter = jnp.exp(inter - m_t)
        qk = jnp.einsum('bhtd,bhsd->bhts', qc, kc) * w_intra
        num = (jnp.einsum('bhts,bhse->bhte', qk, vc)
               + w_inter[..., None] * jnp.einsum('bhed,bhtd->bhte', c_mat, qc))
        den = jnp.sum(qk, axis=-1) + w_inter * jnp.einsum('bhd,bhtd->bht', n_vec, qc)
        h_out = num / jnp.maximum(jnp.abs(den), jnp.exp(-m_t))[..., None]
        total = cum[..., -1]
        w_src = total[..., None] - cum + ic
        m_new = jnp.maximum(total + m_prev, jnp.max(w_src, axis=-1))
        carry_scale = jnp.exp(total + m_prev - m_new)
        w_src = jnp.exp(w_src - m_new[..., None])
        c_new = carry_scale[..., None, None] * c_mat + jnp.einsum('bhs,bhse,bhsd->bhed', w_src, vc, kc)
        n_new = carry_scale[..., None] * n_vec + jnp.einsum('bhs,bhsd->bhd', w_src, kc)
        return (c_new, n_new, m_new), h_out

    init = (jnp.zeros((b, h, dh, dh), jnp.float32),
            jnp.zeros((b, h, dh), jnp.float32),
            jnp.full((b, h), NEG_BIG, jnp.float32))
    _, hs = lax.scan(step, init, (chunks(q), chunks(k), chunks(v), chunks(i_pre), chunks(log_f)))
    return jnp.moveaxis(hs, 0, 2).reshape(b, h, s, dh)


def token_mixer(h, w_in, conv_w, conv_b, gate_bias, mlstm_norm_gain, w_out):
    b, s, _ = h.shape
    proj = h @ w_in
    sizes = [ATT_WIDTH] * 3 + [MLSTM_WIDTH] * 4 + [N_GATES * MLSTM_HEADS]
    cuts = np.cumsum(sizes)[:-1].tolist()
    aq, ak, av, mq, mk, mv, mo, mg = jnp.split(proj, cuts, axis=-1)

    positions = jnp.arange(s)
    aq = partial_rotary(split_heads(aq, ATT_HEADS), positions) * (ATT_HEAD_DIM ** -0.5)
    ak = partial_rotary(split_heads(ak, ATT_HEADS), positions)
    av = split_heads(av, ATT_HEADS)
    attn_out = merge_heads(dilated_attention_mixture(aq, ak, av))

    qk = jnp.concatenate([mq, mk], axis=-1)
    qk = lax.conv_general_dilated(qk, conv_w.reshape(CONV_WIDTH, 1, 2 * MLSTM_WIDTH).astype(qk.dtype),
                                  window_strides=(1,), padding='SAME',
                                  dimension_numbers=('NWC', 'WIO', 'NWC'),
                                  feature_group_count=2 * MLSTM_WIDTH)
    qk = jax.nn.silu(qk + conv_b)
    mq, mk = jnp.split(qk, 2, axis=-1)
    mq = split_heads(mq, MLSTM_HEADS).astype(jnp.float32)
    mk = split_heads(mk, MLSTM_HEADS).astype(jnp.float32) * (MLSTM_HEAD_DIM ** -0.5)
    mv = split_heads(mv, MLSTM_HEADS).astype(jnp.float32)
    gates = mg.astype(jnp.float32).reshape(b, s, N_GATES, MLSTM_HEADS) + gate_bias
    gates = gates.transpose(2, 0, 3, 1)
    h_fwd = mlstm_direction(mq, mk, mv, gates[0], gates[1])
    flip = lambda t: jnp.flip(t, axis=2)
    h_bwd = flip(mlstm_direction(flip(mq), flip(mk), flip(mv), flip(gates[2]), flip(gates[3])))
    cell = h_fwd + h_bwd
    cell = cell * lax.rsqrt(jnp.mean(cell * cell, axis=-1, keepdims=True) + NORM_EPS)
    mlstm_out = jax.nn.sigmoid(mo.astype(jnp.float32)) * (merge_heads(cell) * mlstm_norm_gain)

    merged = jnp.concatenate([attn_out, mlstm_out], axis=-1).astype(h.dtype)
    return merged @ w_out


def setup_inputs(seed: int = 0) -> dict:
    key = jax.random.key(seed)
    ks = jax.random.split(key, 12)
    f32 = jnp.float32
    x = jax.random.normal(ks[0], (BATCH, SEQ, D_MODEL), f32)
    norm_gain = 1.0 + 0.05 * jax.random.normal(ks[1], (DEPTH, N_NORMS, D_MODEL), f32)
    ffn_w_in = jax.random.normal(ks[2], (DEPTH, 2, D_MODEL, 2 * D_FF), f32) * D_MODEL ** -0.5
    ffn_w_out = jax.random.normal(ks[3], (DEPTH, 2, D_FF, D_MODEL), f32) * D_FF ** -0.5
    mix_w_in = jax.random.normal(ks[4], (DEPTH, D_MODEL, IN_WIDTH), f32) * D_MODEL ** -0.5
    conv_w = jax.random.normal(ks[5], (DEPTH, CONV_WIDTH, 2 * MLSTM_WIDTH), f32) * CONV_WIDTH ** -0.5
    conv_b = 0.01 * jax.random.normal(ks[6], (DEPTH, 2 * MLSTM_WIDTH), f32)
    i_bias = 0.1 * jax.random.normal(ks[7], (DEPTH, 2, MLSTM_HEADS), f32)
    f_bias = (jnp.linspace(3.0, 6.0, MLSTM_HEADS, dtype=f32)[None, None, :]
              + 0.1 * jax.random.normal(ks[8], (DEPTH, 2, MLSTM_HEADS), f32))
    gate_bias = jnp.stack([i_bias[:, 0], f_bias[:, 0], i_bias[:, 1], f_bias[:, 1]], axis=1)
    mlstm_norm_gain = 1.0 + 0.05 * jax.random.normal(ks[9], (DEPTH, MLSTM_WIDTH), f32)
    mix_w_out = jax.random.normal(ks[10], (DEPTH, D_MIX, D_MODEL), f32) * D_MIX ** -0.5
    return {"x": x, "norm_gain": norm_gain, "ffn_w_in": ffn_w_in, "ffn_w_out": ffn_w_out,
            "mix_w_in": mix_w_in, "conv_w": conv_w, "conv_b": conv_b, "gate_bias": gate_bias,
            "mlstm_norm_gain": mlstm_norm_gain, "mix_w_out": mix_w_out}


def reference(x, norm_gain, ffn_w_in, ffn_w_out, mix_w_in, conv_w, conv_b, gate_bias,
              mlstm_norm_gain, mix_w_out):
    for layer in range(DEPTH):
        g = norm_gain[layer]
        h = swiglu(rms_norm(x, g[0]), ffn_w_in[layer, 0], ffn_w_out[layer, 0])
        x = x + 0.5 * rms_norm(h, g[1])
        h = token_mixer(rms_norm(x, g[2]), mix_w_in[layer], conv_w[layer], conv_b[layer],
                        gate_bias[layer], mlstm_norm_gain[layer], mix_w_out[layer])
        x = x + rms_norm(h, g[3])
        h = swiglu(rms_norm(x, g[4]), ffn_w_in[layer, 1], ffn_w_out[layer, 1])
        x = x + 0.5 * rms_norm(h, g[5])
    return x
```

```python
import functools

import jax
import jax.numpy as jnp
import numpy as np
from jax import lax
from jax.experimental import pallas as pl
from jax.experimental.pallas import tpu as pltpu

F32 = jnp.float32
BF16 = jnp.bfloat16

D_MODEL = 1024
ATT_HEAD_DIM = 64
ATT_WIDTH = 512
ATT_HEADS = 8
DILATIONS = (1, 4, 16)
N_SIDE = 64
ROPE_THETA = 500000.0
ROPE_DIM = 16
MLSTM_WIDTH = 512
MLSTM_HEADS = 4
MLSTM_HEAD_DIM = 128
MLSTM_CHUNK = 64
CONV_WIDTH = 5
N_GATE_COLS = 16
IN_MAIN = 3 * ATT_WIDTH + 4 * MLSTM_WIDTH
LANES = 128
IN_PAD = IN_MAIN + LANES
D_FF = 2816
NORM_EPS = 1e-6
NEG_BIG = -1e30

COL_AQ, COL_AK, COL_AV = 0, 4, 8
COL_MQ, COL_MK, COL_MV, COL_MO, COL_MG = 12, 16, 20, 24, 28

VMEM_LIMIT = 56 * 1024 * 1024

FFN_TM = 512
FFN_TF = 1408
MIX_TM = 512
ATT_QB = 128
ATT_KB = ATT_QB + 2 * N_SIDE
CONV_ROWS = 256
CONV_HALO = 8


def _rms(x):
    return x * lax.rsqrt(jnp.mean(x * x, axis=-1, keepdims=True) + NORM_EPS)


def _dot_nt(a, b):
    return lax.dot_general(a, b, (((1,), (1,)), ((), ())), preferred_element_type=F32)


def _dot_tn(a, b):
    return lax.dot_general(a, b, (((0,), (0,)), ((), ())), preferred_element_type=F32)


def _ffn_body(x_ref, gpre_ref, gpost_ref, wg_ref, wu_ref, wo_ref, o_ref, hn_ref, acc_ref):
    j = pl.program_id(1)

    @pl.when(j == 0)
    def _():
        hn_ref[...] = (_rms(x_ref[...]) * gpre_ref[...]).astype(BF16)
        acc_ref[...] = jnp.zeros_like(acc_ref)

    hn = hn_ref[...]
    g = jnp.dot(hn, wg_ref[...], preferred_element_type=F32)
    u = jnp.dot(hn, wu_ref[...], preferred_element_type=F32)
    a = (g * jax.nn.sigmoid(g) * u).astype(BF16)
    acc_ref[...] += jnp.dot(a, wo_ref[...], preferred_element_type=F32)

    @pl.when(j == pl.num_programs(1) - 1)
    def _():
        o_ref[...] = x_ref[...] + 0.5 * (_rms(acc_ref[...]) * gpost_ref[...])


def _ffn(x, gpre, gpost, w_in, w_out, layer, which):
    t = x.shape[0]
    nf = D_FF // FFN_TF
    return pl.pallas_call(
        _ffn_body,
        grid=(t // FFN_TM, nf),
        in_specs=[
            pl.BlockSpec((FFN_TM, D_MODEL), lambda i, j: (i, 0)),
            pl.BlockSpec((1, D_MODEL), lambda i, j: (0, 0)),
            pl.BlockSpec((1, D_MODEL), lambda i, j: (0, 0)),
            pl.BlockSpec((None, None, D_MODEL, FFN_TF), lambda i, j: (layer, which, 0, j)),
            pl.BlockSpec((None, None, D_MODEL, FFN_TF), lambda i, j: (layer, which, 0, nf + j)),
            pl.BlockSpec((None, None, FFN_TF, D_MODEL), lambda i, j: (layer, which, j, 0)),
        ],
        out_specs=pl.BlockSpec((FFN_TM, D_MODEL), lambda i, j: (i, 0)),
        out_shape=jax.ShapeDtypeStruct((t, D_MODEL), F32),
        scratch_shapes=[pltpu.VMEM((FFN_TM, D_MODEL), BF16), pltpu.VMEM((FFN_TM, D_MODEL), F32)],
        compiler_params=pltpu.CompilerParams(
            dimension_semantics=("parallel", "arbitrary"), vmem_limit_bytes=VMEM_LIMIT),
        name="ffn",
    )(x, gpre, gpost, w_in, w_in, w_out)


def _mix_in_body(x_ref, g_ref, w_ref, cos_ref, sa_ref, sb_ref, o_ref):
    hn = (_rms(x_ref[...]) * g_ref[...]).astype(BF16)
    cos, sa, sb = cos_ref[...], sa_ref[...], sb_ref[...]

    def rotary(t):
        return t * cos + pltpu.roll(t, LANES - ROPE_DIM // 2, 1) * sa + pltpu.roll(t, ROPE_DIM // 2, 1) * sb

    for c0 in range(0, IN_PAD, ATT_WIDTH):
        c1 = min(c0 + ATT_WIDTH, IN_PAD)
        res = jnp.dot(hn, w_ref[:, c0:c1], preferred_element_type=F32)
        if c0 < 2 * ATT_WIDTH:
            scale = ATT_HEAD_DIM ** -0.5 if c0 == 0 else 1.0
            for s in range(0, ATT_WIDTH, LANES):
                o_ref[:, c0 + s:c0 + s + LANES] = rotary(res[:, s:s + LANES]) * scale
        else:
            o_ref[:, c0:c1] = res


def _mix_in(x, gain, w, cos, sa, sb, seq):
    t = x.shape[0]
    nseq = seq // MIX_TM
    tab = pl.BlockSpec((MIX_TM, LANES), lambda i: (i % nseq, 0))
    return pl.pallas_call(
        _mix_in_body,
        grid=(t // MIX_TM,),
        in_specs=[
            pl.BlockSpec((MIX_TM, D_MODEL), lambda i: (i, 0)),
            pl.BlockSpec((1, D_MODEL), lambda i: (0, 0)),
            pl.BlockSpec((D_MODEL, IN_PAD), lambda i: (0, 0)),
            tab, tab, tab,
        ],
        out_specs=pl.BlockSpec((MIX_TM, IN_PAD), lambda i: (i, 0)),
        out_shape=jax.ShapeDtypeStruct((t, IN_PAD), F32),
        compiler_params=pltpu.CompilerParams(
            dimension_semantics=("parallel",), vmem_limit_bytes=VMEM_LIMIT),
        name="mix_in",
    )(x, gain, w, cos, sa, sb)


def _attn_body(q_ref, k_ref, v_ref, o_ref, *scr):
    seq = q_ref.shape[0]
    o_scr, l_scr = scr[:3], scr[3:]

    for gi, d in enumerate(DILATIONS):
        n = seq // d
        nblk = n // ATT_QB

        def rows(start, size, d=d):
            return pl.ds(start, size) if d == 1 else pl.ds(start, size, stride=d)

        def body(t, carry, d=d, n=n, nblk=nblk, gi=gi, rows=rows):
            r = t // nblk
            m0 = (t % nblk) * ATT_QB
            ks = jnp.clip(m0 - N_SIDE, 0, n - ATT_KB)
            q = q_ref[rows(r + m0 * d, ATT_QB), :]
            k = k_ref[rows(r + ks * d, ATT_KB), :]
            v = v_ref[rows(r + ks * d, ATT_KB), :]
            mq = m0 + lax.broadcasted_iota(jnp.int32, (ATT_QB, ATT_KB), 0)
            mk = ks + lax.broadcasted_iota(jnp.int32, (ATT_QB, ATT_KB), 1)
            valid = jnp.abs(mq - mk) <= N_SIDE
            outs, lses = [], []
            for h in range(2):
                sl = slice(h * ATT_HEAD_DIM, (h + 1) * ATT_HEAD_DIM)
                s = _dot_nt(q[:, sl].astype(BF16), k[:, sl].astype(BF16))
                s = jnp.where(valid, s, NEG_BIG)
                mx = jnp.max(s, axis=1, keepdims=True)
                p = jnp.exp(s - mx)
                l = jnp.sum(p, axis=1, keepdims=True)
                o = jnp.dot(p.astype(BF16), v[:, sl].astype(BF16), preferred_element_type=F32)
                outs.append(o / l)
                lses.append(jnp.broadcast_to(mx + jnp.log(l), (ATT_QB, ATT_HEAD_DIM)))
            o_scr[gi][rows(r + m0 * d, ATT_QB), :] = jnp.concatenate(outs, axis=1)
            l_scr[gi][rows(r + m0 * d, ATT_QB), :] = jnp.concatenate(lses, axis=1)
            return carry

        lax.fori_loop(0, d * nblk, body, 0)

    def merge(c, carry):
        rs = pl.ds(pl.multiple_of(c * ATT_QB, ATT_QB), ATT_QB)
        ls = [l_scr[g][rs, :] for g in range(3)]
        mx = jnp.maximum(jnp.maximum(ls[0], ls[1]), ls[2])
        ws = [jnp.exp(l - mx) for l in ls]
        den = ws[0] + ws[1] + ws[2]
        acc = ws[0] * o_scr[0][rs, :] + ws[1] * o_scr[1][rs, :] + ws[2] * o_scr[2][rs, :]
        o_ref[rs, :] = acc / den
        return carry

    lax.fori_loop(0, seq // ATT_QB, merge, 0)


def _attn(proj, batch, seq):
    blk = lambda col: pl.BlockSpec((None, seq, LANES), lambda b, p: (b, 0, col + p))
    return pl.pallas_call(
        _attn_body,
        grid=(batch, ATT_HEADS // 2),
        in_specs=[blk(COL_AQ), blk(COL_AK), blk(COL_AV)],
        out_specs=pl.BlockSpec((None, seq, LANES), lambda b, p: (b, 0, p)),
        out_shape=jax.ShapeDtypeStruct((batch, seq, ATT_WIDTH), F32),
        scratch_shapes=[pltpu.VMEM((seq, LANES), F32) for _ in range(6)],
        compiler_params=pltpu.CompilerParams(
            dimension_semantics=("parallel", "parallel"), vmem_limit_bytes=VMEM_LIMIT),
        name="attn",
    )(proj, proj, proj)


def _log_sigmoid(x):
    return jnp.minimum(x, 0.0) - jnp.log1p(jnp.exp(-jnp.abs(x)))


def _split3(x):
    hi = x.astype(BF16).astype(F32)
    r = x - hi
    mid = r.astype(BF16).astype(F32)
    return hi, mid, r - mid


def _mlstm_body(q_ref, k_ref, v_ref, og_ref, g_ref, gb_ref, cwq_ref, cwk_ref, cbq_ref, cbk_ref,
                gain_ref, o_ref, pad_scr, qs, ks, vs, a_f, b_f, a_b, b_b, colq, h_f, h_b):
    seq = q_ref.shape[0]
    L = MLSTM_CHUNK
    nc = seq // L
    head = pl.program_id(1)
    lane = lax.broadcasted_iota(jnp.int32, (1, LANES), 1)

    def conv(src_ref, w_ref, b_ref, dst_ref, scale):
        pad_scr[0:CONV_HALO, :] = jnp.zeros((CONV_HALO, LANES), F32)
        pad_scr[CONV_HALO + seq:CONV_HALO + seq + CONV_HALO, :] = jnp.zeros((CONV_HALO, LANES), F32)
        pad_scr[CONV_HALO:CONV_HALO + seq, :] = src_ref[...]
        w = w_ref[...]
        b = b_ref[...]

        def chunk(c, carry):
            start = pl.multiple_of(c * CONV_ROWS, CONV_ROWS)
            win = pad_scr[pl.ds(start, CONV_ROWS + 2 * CONV_HALO), :]
            acc = b
            for j in range(CONV_WIDTH):
                off = CONV_HALO - CONV_WIDTH // 2 + j
                acc = acc + w[j:j + 1, :] * win[off:off + CONV_ROWS, :]
            y = acc * jax.nn.sigmoid(acc) * scale
            dst_ref[pl.ds(start, CONV_ROWS), :] = y.astype(BF16)
            return carry

        lax.fori_loop(0, seq // CONV_ROWS, chunk, 0)

    conv(q_ref, cwq_ref, cbq_ref, qs, 1.0)
    conv(k_ref, cwk_ref, cbk_ref, ks, MLSTM_HEAD_DIM ** -0.5)
    vs[...] = v_ref[...].astype(BF16)

    pos = lax.broadcasted_iota(jnp.int32, (L, LANES), 0)

    def colsel(x, j):
        return jnp.sum(jnp.where(lane == j, x, 0.0), axis=1, keepdims=True)

    def factors(cum, b):
        ch, cm, cl = _split3(cum)
        bh, bm, bl = _split3(b)
        one = jnp.where((lane >= 3) & (lane < 6), 1.0, 0.0)
        a = jnp.where(lane == 0, ch, jnp.where(lane == 1, cm, jnp.where(lane == 2, cl, one)))
        one_b = jnp.where(lane < 3, 1.0, 0.0)
        bb = jnp.where(lane == 3, bh, jnp.where(lane == 4, bm, jnp.where(lane == 5, bl, one_b)))
        return a.astype(BF16), bb.astype(BF16)

    def gates(c, carry):
        rs = pl.ds(pl.multiple_of(c * L, L), L)
        gb = g_ref[rs, :] + gb_ref[...]
        lf = _log_sigmoid(gb)
        pf, sf = lf, lf
        sh = 1
        while sh < L:
            pf = pf + jnp.where(pos >= sh, pltpu.roll(pf, sh, 0), 0.0)
            sf = sf + jnp.where(pos < L - sh, pltpu.roll(sf, L - sh, 0), 0.0)
            sh *= 2
        i_f = colsel(gb, head)
        cum_f = colsel(pf, MLSTM_HEADS + head)
        src_f = colsel(sf - lf, MLSTM_HEADS + head) + i_f
        i_b = colsel(gb, 2 * MLSTM_HEADS + head)
        cum_b = colsel(sf, 3 * MLSTM_HEADS + head)
        src_b = colsel(pf - lf, 3 * MLSTM_HEADS + head) + i_b
        a, b = factors(cum_f, i_f - cum_f)
        a_f[rs, :] = a
        b_f[rs, :] = b
        a, b = factors(cum_b, i_b - cum_b)
        a_b[rs, :] = a
        b_b[rs, :] = b
        colq[rs, :] = jnp.where(lane == 0, cum_f, jnp.where(lane == 1, src_f,
                                jnp.where(lane == 2, cum_b, jnp.where(lane == 3, src_b, 0.0))))
        return carry

    lax.fori_loop(0, nc, gates, 0)

    ti = lax.broadcasted_iota(jnp.int32, (L, L), 0)
    si = lax.broadcasted_iota(jnp.int32, (L, L), 1)

    def step(c, a_ref, b_ref, lane0, tot_row, mask, h_ref, state):
        c_mat, n_vec, m_prev = state
        rs = pl.ds(pl.multiple_of(c * L, L), L)
        q, k, v = qs[rs, :], ks[rs, :], vs[rs, :]
        cols = colq[rs, :]
        cum = cols[:, lane0:lane0 + 1]
        src = cols[:, lane0 + 1:lane0 + 2]
        decay = jnp.where(mask, _dot_nt(a_ref[rs, :], b_ref[rs, :]), -jnp.inf)
        inter = cum + m_prev
        m_t = jnp.maximum(inter, jnp.max(decay, axis=1, keepdims=True))
        w_intra = jnp.exp(decay - m_t)
        w_inter = jnp.exp(inter - m_t)
        qk = _dot_nt(q, k) * w_intra
        num = (jnp.dot(qk.astype(BF16), v, preferred_element_type=F32)
               + w_inter * _dot_nt(q, c_mat.astype(BF16)))
        den = (jnp.sum(qk, axis=1, keepdims=True)
               + w_inter * jnp.sum(q.astype(F32) * n_vec, axis=1, keepdims=True))
        h_ref[rs, :] = num / jnp.maximum(jnp.abs(den), jnp.exp(-m_t))
        total = cum[tot_row:tot_row + 1, :]
        m_new = jnp.maximum(total + m_prev, jnp.max(src, axis=0, keepdims=True))
        carry_scale = jnp.exp(total + m_prev - m_new)
        w_src = jnp.exp(src - m_new)
        vw = (v.astype(F32) * w_src).astype(BF16)
        c_new = carry_scale * c_mat + _dot_tn(vw, k)
        n_new = carry_scale * n_vec + jnp.sum(k.astype(F32) * w_src, axis=0, keepdims=True)
        return c_new, n_new, m_new

    def both(c, carry):
        st_f, st_b = carry
        st_f = step(c, a_f, b_f, 0, L - 1, si <= ti, h_f, st_f)
        st_b = step(nc - 1 - c, a_b, b_b, 2, 0, si >= ti, h_b, st_b)
        return st_f, st_b

    init = (jnp.zeros((MLSTM_HEAD_DIM, MLSTM_HEAD_DIM), F32), jnp.zeros((1, MLSTM_HEAD_DIM), F32),
            jnp.full((1, 1), NEG_BIG, F32))
    lax.fori_loop(0, nc, both, (init, init))

    def finish(c, carry):
        rs = pl.ds(pl.multiple_of(c * CONV_ROWS, CONV_ROWS), CONV_ROWS)
        cell = h_f[rs, :] + h_b[rs, :]
        o_ref[rs, :] = jax.nn.sigmoid(og_ref[rs, :]) * (_rms(cell) * gain_ref[...])
        return carry

    lax.fori_loop(0, seq // CONV_ROWS, finish, 0)


def _mlstm(proj, gate_bias_row, conv_w, conv_b, gain, batch, seq):
    blk = lambda col: pl.BlockSpec((None, seq, LANES), lambda b, h: (b, 0, col + h))
    row = lambda off: pl.BlockSpec((1, LANES), lambda b, h: (0, off + h))
    tap = lambda off: pl.BlockSpec((CONV_WIDTH, LANES), lambda b, h: (0, off + h))
    return pl.pallas_call(
        _mlstm_body,
        grid=(batch, MLSTM_HEADS),
        in_specs=[
            blk(COL_MQ), blk(COL_MK), blk(COL_MV), blk(COL_MO),
            pl.BlockSpec((None, seq, LANES), lambda b, h: (b, 0, COL_MG)),
            pl.BlockSpec((1, LANES), lambda b, h: (0, 0)),
            tap(0), tap(MLSTM_HEADS), row(0), row(MLSTM_HEADS), row(0),
        ],
        out_specs=pl.BlockSpec((None, seq, LANES), lambda b, h: (b, 0, h)),
        out_shape=jax.ShapeDtypeStruct((batch, seq, MLSTM_WIDTH), F32),
        scratch_shapes=[
            pltpu.VMEM((seq + 2 * CONV_HALO, LANES), F32),
            pltpu.VMEM((seq, LANES), BF16), pltpu.VMEM((seq, LANES), BF16), pltpu.VMEM((seq, LANES), BF16),
            pltpu.VMEM((seq, LANES), BF16), pltpu.VMEM((seq, LANES), BF16),
            pltpu.VMEM((seq, LANES), BF16), pltpu.VMEM((seq, LANES), BF16),
            pltpu.VMEM((seq, LANES), F32), pltpu.VMEM((seq, LANES), F32), pltpu.VMEM((seq, LANES), F32),
        ],
        compiler_params=pltpu.CompilerParams(
            dimension_semantics=("parallel", "parallel"), vmem_limit_bytes=VMEM_LIMIT),
        name="mlstm",
    )(proj, proj, proj, proj, proj, gate_bias_row, conv_w, conv_w, conv_b, conv_b, gain)


def _mix_out_body(x_ref, a_ref, m_ref, wa_ref, wm_ref, g_ref, o_ref):
    h = (jnp.dot(a_ref[...].astype(BF16), wa_ref[...], preferred_element_type=F32)
         + jnp.dot(m_ref[...].astype(BF16), wm_ref[...], preferred_element_type=F32))
    o_ref[...] = x_ref[...] + _rms(h) * g_ref[...]


def _mix_out(x, attn, ml, w_out, gain, layer):
    t = x.shape[0]
    return pl.pallas_call(
        _mix_out_body,
        grid=(t // MIX_TM,),
        in_specs=[
            pl.BlockSpec((MIX_TM, D_MODEL), lambda i: (i, 0)),
            pl.BlockSpec((MIX_TM, ATT_WIDTH), lambda i: (i, 0)),
            pl.BlockSpec((MIX_TM, MLSTM_WIDTH), lambda i: (i, 0)),
            pl.BlockSpec((None, None, ATT_WIDTH, D_MODEL), lambda i: (layer, 0, 0, 0)),
            pl.BlockSpec((None, None, MLSTM_WIDTH, D_MODEL), lambda i: (layer, 1, 0, 0)),
            pl.BlockSpec((1, D_MODEL), lambda i: (0, 0)),
        ],
        out_specs=pl.BlockSpec((MIX_TM, D_MODEL), lambda i: (i, 0)),
        out_shape=jax.ShapeDtypeStruct((t, D_MODEL), F32),
        compiler_params=pltpu.CompilerParams(
            dimension_semantics=("parallel",), vmem_limit_bytes=VMEM_LIMIT),
        name="mix_out",
    )(x, attn, ml, w_out, w_out, gain)


def _rope_tables(seq):
    half = ROPE_DIM // 2
    inv_freq = ROPE_THETA ** (-jnp.arange(half, dtype=F32) / half)
    ang = jnp.arange(seq, dtype=F32)[:, None] * inv_freq[None, :]
    cos, sin = jnp.cos(ang), jnp.sin(ang)
    pad = jnp.zeros((seq, ATT_HEAD_DIM - ROPE_DIM), F32)
    zero = jnp.zeros((seq, half), F32)
    cos_t = jnp.concatenate([cos, cos, pad + 1.0], axis=1)
    sa_t = jnp.concatenate([-sin, zero, pad], axis=1)
    sb_t = jnp.concatenate([zero, sin, pad], axis=1)
    reps = LANES // ATT_HEAD_DIM
    return tuple(jnp.tile(t, (1, reps)) for t in (cos_t, sa_t, sb_t))


def kernel(x, norm_gain, ffn_w_in, ffn_w_out, mix_w_in, conv_w, conv_b, gate_bias, mlstm_norm_gain, mix_w_out):
    batch, seq, _ = x.shape
    depth = norm_gain.shape[0]
    w_in = ffn_w_in.astype(BF16)
    w_out = ffn_w_out.astype(BF16)
    mw_in = jnp.pad(mix_w_in, ((0, 0), (0, 0), (0, IN_PAD - mix_w_in.shape[2]))).astype(BF16)
    mw_out = mix_w_out.astype(BF16)
    gb_rows = jnp.pad(gate_bias.reshape(depth, 1, N_GATE_COLS), ((0, 0), (0, 0), (0, LANES - N_GATE_COLS)))
    cos, sa, sb = _rope_tables(seq)

    xt = x.reshape(batch * seq, D_MODEL)
    for layer in range(depth):
        g = norm_gain[layer][:, None, :]
        xt = _ffn(xt, g[0], g[1], w_in, w_out, layer, 0)
        proj = _mix_in(xt, g[2], mw_in[layer], cos, sa, sb, seq).reshape(batch, seq, IN_PAD)
        attn = _attn(proj, batch, seq)
        ml = _mlstm(proj, gb_rows[layer], conv_w[layer], conv_b[layer][None, :],
                    mlstm_norm_gain[layer][None, :], batch, seq)
        xt = _mix_out(xt, attn.reshape(batch * seq, ATT_WIDTH), ml.reshape(batch * seq, MLSTM_WIDTH),
                      mw_out.reshape(depth, 2, ATT_WIDTH, D_MODEL), g[3], layer)
        xt = _ffn(xt, g[4], g[5], w_in, w_out, layer, 1)
    return xt.reshape(batch, seq, D_MODEL)
```

```python
import jax
import jax.numpy as jnp
from jax import lax
from jax.experimental import pallas as pl
from jax.experimental.pallas import tpu as pltpu

F32 = jnp.float32
BF16 = jnp.bfloat16

D_MODEL = 1024
ATT_HEAD_DIM = 64
ATT_WIDTH = 512
ATT_HEADS = 8
DILATIONS = (1, 4, 16)
N_SIDE = 64
ROPE_THETA = 500000.0
ROPE_DIM = 16
MLSTM_WIDTH = 512
MLSTM_HEADS = 4
MLSTM_HEAD_DIM = 128
MLSTM_CHUNK = 64
CONV_WIDTH = 5
N_GATE_COLS = 16
IN_MAIN = 3 * ATT_WIDTH + 4 * MLSTM_WIDTH
LANES = 128
IN_PAD = IN_MAIN + LANES
D_FF = 2816
NORM_EPS = 1e-6
NEG_BIG = -1e30

COL_AQ, COL_AK, COL_AV = 0, 4, 8
COL_MQ, COL_MK, COL_MV, COL_MO, COL_MG = 12, 16, 20, 24, 28

VMEM_LIMIT = 56 * 1024 * 1024

FFN_TM = 512
FFN_TF = 1408
MIX_TM = 512
ATT_QB = 128
ATT_KB = ATT_QB + 2 * N_SIDE
ATT_REGROUP = 256
ATT_GROUP = 4
MLSTM_GROUP = 4
CONV_ROWS = 256
CONV_HALO = 8


def _rms(x):
    return x * lax.rsqrt(jnp.mean(x * x, axis=-1, keepdims=True) + NORM_EPS)


def _dot_nt(a, b):
    return lax.dot_general(a, b, (((1,), (1,)), ((), ())), preferred_element_type=F32)


def _dot_tn(a, b):
    return lax.dot_general(a, b, (((0,), (0,)), ((), ())), preferred_element_type=F32)


def _ffn_body(x_ref, gpre_ref, gpost_ref, wg_ref, wu_ref, wo_ref, o_ref, hn_ref, acc_ref):
    j = pl.program_id(1)

    @pl.when(j == 0)
    def _():
        hn_ref[...] = (_rms(x_ref[...]) * gpre_ref[...]).astype(BF16)
        acc_ref[...] = jnp.zeros_like(acc_ref)

    hn = hn_ref[...]
    g = jnp.dot(hn, wg_ref[...], preferred_element_type=F32)
    u = jnp.dot(hn, wu_ref[...], preferred_element_type=F32)
    a = (g * jax.nn.sigmoid(g) * u).astype(BF16)
    acc_ref[...] += jnp.dot(a, wo_ref[...], preferred_element_type=F32)

    @pl.when(j == pl.num_programs(1) - 1)
    def _():
        o_ref[...] = x_ref[...] + 0.5 * (_rms(acc_ref[...]) * gpost_ref[...])


def _ffn(x, gpre, gpost, w_in, w_out, layer, which):
    t = x.shape[0]
    nf = D_FF // FFN_TF
    return pl.pallas_call(
        _ffn_body,
        grid=(t // FFN_TM, nf),
        in_specs=[
            pl.BlockSpec((FFN_TM, D_MODEL), lambda i, j: (i, 0)),
            pl.BlockSpec((1, D_MODEL), lambda i, j: (0, 0)),
            pl.BlockSpec((1, D_MODEL), lambda i, j: (0, 0)),
            pl.BlockSpec((None, None, D_MODEL, FFN_TF), lambda i, j: (layer, which, 0, j)),
            pl.BlockSpec((None, None, D_MODEL, FFN_TF), lambda i, j: (layer, which, 0, nf + j)),
            pl.BlockSpec((None, None, FFN_TF, D_MODEL), lambda i, j: (layer, which, j, 0)),
        ],
        out_specs=pl.BlockSpec((FFN_TM, D_MODEL), lambda i, j: (i, 0)),
        out_shape=jax.ShapeDtypeStruct((t, D_MODEL), F32),
        scratch_shapes=[pltpu.VMEM((FFN_TM, D_MODEL), BF16), pltpu.VMEM((FFN_TM, D_MODEL), F32)],
        compiler_params=pltpu.CompilerParams(
            dimension_semantics=("parallel", "arbitrary"), vmem_limit_bytes=VMEM_LIMIT),
        name="ffn",
    )(x, gpre, gpost, w_in, w_in, w_out)


def _mix_in_body(x_ref, g_ref, w_ref, cos_ref, sa_ref, sb_ref, o_ref):
    hn = (_rms(x_ref[...]) * g_ref[...]).astype(BF16)
    cos, sa, sb = cos_ref[...], sa_ref[...], sb_ref[...]

    def rotary(t):
        return t * cos + pltpu.roll(t, LANES - ROPE_DIM // 2, 1) * sa + pltpu.roll(t, ROPE_DIM // 2, 1) * sb

    for c0 in range(0, IN_PAD, ATT_WIDTH):
        c1 = min(c0 + ATT_WIDTH, IN_PAD)
        res = jnp.dot(hn, w_ref[:, c0:c1], preferred_element_type=F32)
        if c0 < 2 * ATT_WIDTH:
            scale = ATT_HEAD_DIM ** -0.5 if c0 == 0 else 1.0
            for s in range(0, ATT_WIDTH, LANES):
                o_ref[:, c0 + s:c0 + s + LANES] = rotary(res[:, s:s + LANES]) * scale
        else:
            o_ref[:, c0:c1] = res


def _mix_in(x, gain, w, cos, sa, sb, seq):
    t = x.shape[0]
    nseq = seq // MIX_TM
    tab = pl.BlockSpec((MIX_TM, LANES), lambda i: (i % nseq, 0))
    return pl.pallas_call(
        _mix_in_body,
        grid=(t // MIX_TM,),
        in_specs=[
            pl.BlockSpec((MIX_TM, D_MODEL), lambda i: (i, 0)),
            pl.BlockSpec((1, D_MODEL), lambda i: (0, 0)),
            pl.BlockSpec((D_MODEL, IN_PAD), lambda i: (0, 0)),
            tab, tab, tab,
        ],
        out_specs=pl.BlockSpec((MIX_TM, IN_PAD), lambda i: (i, 0)),
        out_shape=jax.ShapeDtypeStruct((t, IN_PAD), F32),
        compiler_params=pltpu.CompilerParams(
            dimension_semantics=("parallel",), vmem_limit_bytes=VMEM_LIMIT),
        name="mix_in",
    )(x, gain, w, cos, sa, sb)


def _attn_body(q_ref, k_ref, v_ref, o_ref, qd, kl, kh, vl, vh, *scr):
    seq = q_ref.shape[0]
    o_scr, l_scr = scr[:3], scr[3:]
    low = lax.broadcasted_iota(jnp.int32, (1, LANES), 1) < ATT_HEAD_DIM
    diff = (lax.broadcasted_iota(jnp.int32, (ATT_QB, ATT_KB), 0)
            - lax.broadcasted_iota(jnp.int32, (ATT_QB, ATT_KB), 1))

    for gi, d in enumerate(DILATIONS):
        n = seq // d
        nblk = n // ATT_QB
        nreg = n // ATT_REGROUP

        def rows(start, size, d=d):
            return pl.ds(start, size) if d == 1 else pl.ds(start, size, stride=d)

        def regroup(t, carry, d=d, nreg=nreg, rows=rows):
            src = rows(t // nreg + (t % nreg) * (ATT_REGROUP * d), ATT_REGROUP)
            dst = pl.ds(pl.multiple_of(t * ATT_REGROUP, ATT_REGROUP), ATT_REGROUP)
            qd[dst, :] = q_ref[src, :].astype(BF16)
            k = k_ref[src, :]
            kl[dst, :] = jnp.where(low, k, 0.0).astype(BF16)
            kh[dst, :] = jnp.where(low, 0.0, k).astype(BF16)
            v = v_ref[src, :]
            vl[dst, :] = jnp.where(low, v, 0.0).astype(BF16)
            vh[dst, :] = jnp.where(low, 0.0, v).astype(BF16)
            return carry

        lax.fori_loop(0, seq // ATT_REGROUP, regroup, 0)

        def body(t, carry, d=d, n=n, nblk=nblk, gi=gi, rows=rows):
            blocks = []
            for g in range(ATT_GROUP):
                tg = t * ATT_GROUP + g
                r = tg // nblk
                m0 = (tg % nblk) * ATT_QB
                ks = jnp.clip(m0 - N_SIDE, 0, n - ATT_KB)
                qrow = pl.ds(pl.multiple_of(r * n + m0, ATT_QB), ATT_QB)
                krow = pl.ds(pl.multiple_of(r * n + ks, N_SIDE), ATT_KB)
                blocks.append((r, m0, ks, qrow, krow))
            scores = []
            for r, m0, ks, qrow, krow in blocks:
                kbd = jnp.concatenate([kl[krow, :], kh[krow, :]], axis=0)
                scores.append(_dot_nt(qd[qrow, :], kbd))
            probs = []
            for (r, m0, ks, qrow, krow), s in zip(blocks, scores):
                off = diff + (m0 - ks + N_SIDE)
                valid = (off >= 0) & (off <= 2 * N_SIDE)
                s0 = jnp.where(valid, s[:, :ATT_KB], NEG_BIG)
                s1 = jnp.where(valid, s[:, ATT_KB:], NEG_BIG)
                mx0 = jnp.max(s0, axis=1, keepdims=True)
                mx1 = jnp.max(s1, axis=1, keepdims=True)
                p0 = jnp.exp(s0 - mx0)
                p1 = jnp.exp(s1 - mx1)
                l0 = jnp.sum(p0, axis=1, keepdims=True)
                l1 = jnp.sum(p1, axis=1, keepdims=True)
                probs.append((jnp.concatenate([p0, p1], axis=1).astype(BF16), mx0, mx1, l0, l1))
            outs = []
            for (r, m0, ks, qrow, krow), (p, mx0, mx1, l0, l1) in zip(blocks, probs):
                vbd = jnp.concatenate([vl[krow, :], vh[krow, :]], axis=0)
                outs.append(jnp.dot(p, vbd, preferred_element_type=F32))
            for (r, m0, ks, qrow, krow), (p, mx0, mx1, l0, l1), o in zip(blocks, probs, outs):
                dst = rows(r + m0 * d, ATT_QB)
                o_scr[gi][dst, :] = o * jnp.where(low, 1.0 / l0, 1.0 / l1)
                l_scr[gi][dst, :] = jnp.where(low, mx0 + jnp.log(l0), mx1 + jnp.log(l1))
            return carry

        lax.fori_loop(0, d * nblk // ATT_GROUP, body, 0)

    def merge(c, carry):
        rs = pl.ds(pl.multiple_of(c * ATT_QB, ATT_QB), ATT_QB)
        ls = [l_scr[g][rs, :] for g in range(3)]
        mx = jnp.maximum(jnp.maximum(ls[0], ls[1]), ls[2])
        ws = [jnp.exp(l - mx) for l in ls]
        den = ws[0] + ws[1] + ws[2]
        acc = ws[0] * o_scr[0][rs, :] + ws[1] * o_scr[1][rs, :] + ws[2] * o_scr[2][rs, :]
        o_ref[rs, :] = acc / den
        return carry

    lax.fori_loop(0, seq // ATT_QB, merge, 0)


def _attn(proj, batch, seq):
    blk = lambda col: pl.BlockSpec((None, seq, LANES), lambda b, p: (b, 0, col + p))
    return pl.pallas_call(
        _attn_body,
        grid=(batch, ATT_HEADS // 2),
        in_specs=[blk(COL_AQ), blk(COL_AK), blk(COL_AV)],
        out_specs=pl.BlockSpec((None, seq, LANES), lambda b, p: (b, 0, p)),
        out_shape=jax.ShapeDtypeStruct((batch, seq, ATT_WIDTH), F32),
        scratch_shapes=([pltpu.VMEM((seq, LANES), BF16) for _ in range(5)]
                        + [pltpu.VMEM((seq, LANES), F32) for _ in range(6)]),
        compiler_params=pltpu.CompilerParams(
            dimension_semantics=("parallel", "parallel"), vmem_limit_bytes=VMEM_LIMIT),
        name="attn",
    )(proj, proj, proj)


def _log_sigmoid(x):
    return jnp.minimum(x, 0.0) - jnp.log1p(jnp.exp(-jnp.abs(x)))


def _split3(x):
    hi = x.astype(BF16).astype(F32)
    r = x - hi
    mid = r.astype(BF16).astype(F32)
    return hi, mid, r - mid


def _mlstm_body(q_ref, k_ref, v_ref, og_ref, g_ref, gb_ref, cwq_ref, cwk_ref, cbq_ref, cbk_ref,
                gain_ref, o_ref, pad_scr, qs, ks, vs, a_f, b_f, a_b, b_b, col_f, col_b, h_f, h_b):
    seq = q_ref.shape[0]
    L = MLSTM_CHUNK
    nc = seq // L
    head = pl.program_id(1)
    lane = lax.broadcasted_iota(jnp.int32, (1, LANES), 1)

    def conv(src_ref, w_ref, b_ref, dst_ref, scale):
        pad_scr[0:CONV_HALO, :] = jnp.zeros((CONV_HALO, LANES), F32)
        pad_scr[CONV_HALO + seq:CONV_HALO + seq + CONV_HALO, :] = jnp.zeros((CONV_HALO, LANES), F32)
        pad_scr[CONV_HALO:CONV_HALO + seq, :] = src_ref[...]
        w = w_ref[...]
        b = b_ref[...]

        def chunk(c, carry):
            start = pl.multiple_of(c * CONV_ROWS, CONV_ROWS)
            win = pad_scr[pl.ds(start, CONV_ROWS + 2 * CONV_HALO), :]
            acc = b
            for j in range(CONV_WIDTH):
                off = CONV_HALO - CONV_WIDTH // 2 + j
                acc = acc + w[j:j + 1, :] * win[off:off + CONV_ROWS, :]
            y = acc * jax.nn.sigmoid(acc) * scale
            dst_ref[pl.ds(start, CONV_ROWS), :] = y.astype(BF16)
            return carry

        lax.fori_loop(0, seq // CONV_ROWS, chunk, 0)

    conv(q_ref, cwq_ref, cbq_ref, qs, 1.0)
    conv(k_ref, cwk_ref, cbk_ref, ks, MLSTM_HEAD_DIM ** -0.5)
    vs[...] = v_ref[...].astype(BF16)

    pos = lax.broadcasted_iota(jnp.int32, (L, LANES), 0)

    def colsel(x, j):
        return jnp.sum(jnp.where(lane == j, x, 0.0), axis=1, keepdims=True)

    def factors(cum, b):
        ch, cm, cl = _split3(cum)
        bh, bm, bl = _split3(b)
        one = jnp.where((lane >= 3) & (lane < 6), 1.0, 0.0)
        a = jnp.where(lane == 0, ch, jnp.where(lane == 1, cm, jnp.where(lane == 2, cl, one)))
        one_b = jnp.where(lane < 3, 1.0, 0.0)
        bb = jnp.where(lane == 3, bh, jnp.where(lane == 4, bm, jnp.where(lane == 5, bl, one_b)))
        return a.astype(BF16), bb.astype(BF16)

    def gate_chunk(c, forward, m_prev, a_ref, b_ref, col_ref):
        rs = pl.ds(pl.multiple_of(c * L, L), L)
        gb = g_ref[rs, :] + gb_ref[...]
        cum = _log_sigmoid(gb)
        sh = 1
        while sh < L:
            if forward:
                cum = cum + jnp.where(pos >= sh, pltpu.roll(cum, sh, 0), 0.0)
            else:
                cum = cum + jnp.where(pos < L - sh, pltpu.roll(cum, L - sh, 0), 0.0)
            sh *= 2
        base = 0 if forward else 2 * MLSTM_HEADS
        i_pre = colsel(gb, base + head)
        cum = colsel(cum, base + MLSTM_HEADS + head)
        b_src = i_pre - cum
        b_max = jnp.broadcast_to(b_src, (L, LANES))
        sh = 1
        while sh < L:
            if forward:
                b_max = jnp.maximum(b_max, jnp.where(pos >= sh, pltpu.roll(b_max, sh, 0), -jnp.inf))
            else:
                b_max = jnp.maximum(b_max, jnp.where(pos < L - sh, pltpu.roll(b_max, L - sh, 0), -jnp.inf))
            sh *= 2
        m_rel = jnp.maximum(m_prev, b_max[:, 0:1])
        m_t = cum + m_rel
        w_inter = jnp.exp(m_prev - m_rel)
        total = cum[L - 1:L, :] if forward else cum[0:1, :]
        src = total - cum + i_pre
        m_new = jnp.maximum(total + m_prev, jnp.max(src, axis=0, keepdims=True))
        carry_scale = jnp.exp(total + m_prev - m_new)
        w_src = jnp.exp(src - m_new)
        a, b = factors(-m_rel, b_src)
        a_ref[rs, :] = a
        b_ref[rs, :] = b
        col_ref[rs, :] = jnp.where(lane == 0, w_src, jnp.where(lane == 1, w_inter,
                                   jnp.where(lane == 2, jnp.exp(-m_t),
                                             jnp.where(lane == 3, carry_scale, 0.0))))
        return m_new

    def gates(c, carry):
        m_f, m_b = carry
        m_f = gate_chunk(c, True, m_f, a_f, b_f, col_f)
        m_b = gate_chunk(nc - 1 - c, False, m_b, a_b, b_b, col_b)
        return m_f, m_b

    m_init = jnp.full((1, 1), NEG_BIG, F32)
    lax.fori_loop(0, nc, gates, (m_init, m_init))

    ti = lax.broadcasted_iota(jnp.int32, (L, L), 0)
    si = lax.broadcasted_iota(jnp.int32, (L, L), 1)

    def both(it, carry):
        states = list(carry)
        tasks = []
        for g in range(MLSTM_GROUP):
            tasks.append((0, it * MLSTM_GROUP + g, a_f, b_f, col_f, si <= ti, h_f))
        for g in range(MLSTM_GROUP):
            tasks.append((1, nc - 1 - (it * MLSTM_GROUP + g), a_b, b_b, col_b, si >= ti, h_b))
        work = []
        for dirn, c, a_ref, b_ref, col_ref, mask, h_ref in tasks:
            rs = pl.ds(pl.multiple_of(c * L, L), L)
            q, k, v = qs[rs, :], ks[rs, :], vs[rs, :]
            cols = col_ref[rs, :]
            w_src = cols[:, 0:1]
            decay = _dot_nt(a_ref[rs, :], b_ref[rs, :])
            qk = _dot_nt(q, k)
            vw = (v.astype(F32) * w_src).astype(BF16)
            upd = _dot_tn(vw, k)
            ksum = jnp.sum(k.astype(F32) * w_src, axis=0, keepdims=True)
            work.append(dict(dirn=dirn, rs=rs, q=q, v=v, cols=cols, decay=decay, qk=qk, upd=upd,
                             ksum=ksum, mask=mask, h_ref=h_ref))
        for w in work:
            c_mat, n_vec = states[w["dirn"]]
            w["c_mat"], w["n_vec"] = c_mat, n_vec
            carry_scale = w["cols"][0:1, 3:4]
            states[w["dirn"]] = (carry_scale * c_mat + w["upd"], carry_scale * n_vec + w["ksum"])
        for w in work:
            w["qc"] = _dot_nt(w["q"], w["c_mat"].astype(BF16))
        for w in work:
            w_intra = jnp.exp(jnp.where(w["mask"], w["decay"], -jnp.inf))
            w["s"] = w["qk"] * w_intra
        for w in work:
            w["sv"] = jnp.dot(w["s"].astype(BF16), w["v"], preferred_element_type=F32)
        for w in work:
            w_inter, e_neg = w["cols"][:, 1:2], w["cols"][:, 2:3]
            num = w["sv"] + w_inter * w["qc"]
            den = (jnp.sum(w["s"], axis=1, keepdims=True)
                   + w_inter * jnp.sum(w["q"].astype(F32) * w["n_vec"], axis=1, keepdims=True))
            w["h_ref"][w["rs"], :] = num / jnp.maximum(jnp.abs(den), e_neg)
        return tuple(states)

    init = (jnp.zeros((MLSTM_HEAD_DIM, MLSTM_HEAD_DIM), F32), jnp.zeros((1, MLSTM_HEAD_DIM), F32))
    lax.fori_loop(0, nc // MLSTM_GROUP, both, (init, init))

    def finish(c, carry):
        rs = pl.ds(pl.multiple_of(c * CONV_ROWS, CONV_ROWS), CONV_ROWS)
        cell = h_f[rs, :] + h_b[rs, :]
        o_ref[rs, :] = jax.nn.sigmoid(og_ref[rs, :]) * (_rms(cell) * gain_ref[...])
        return carry

    lax.fori_loop(0, seq // CONV_ROWS, finish, 0)


def _mlstm(proj, gate_bias_row, conv_w, conv_b, gain, batch, seq):
    blk = lambda col: pl.BlockSpec((None, seq, LANES), lambda b, h: (b, 0, col + h))
    row = lambda off: pl.BlockSpec((1, LANES), lambda b, h: (0, off + h))
    tap = lambda off: pl.BlockSpec((CONV_WIDTH, LANES), lambda b, h: (0, off + h))
    return pl.pallas_call(
        _mlstm_body,
        grid=(batch, MLSTM_HEADS),
        in_specs=[
            blk(COL_MQ), blk(COL_MK), blk(COL_MV), blk(COL_MO),
            pl.BlockSpec((None, seq, LANES), lambda b, h: (b, 0, COL_MG)),
            pl.BlockSpec((1, LANES), lambda b, h: (0, 0)),
            tap(0), tap(MLSTM_HEADS), row(0), row(MLSTM_HEADS), row(0),
        ],
        out_specs=pl.BlockSpec((None, seq, LANES), lambda b, h: (b, 0, h)),
        out_shape=jax.ShapeDtypeStruct((batch, seq, MLSTM_WIDTH), F32),
        scratch_shapes=(
            [pltpu.VMEM((seq + 2 * CONV_HALO, LANES), F32)]
            + [pltpu.VMEM((seq, LANES), BF16) for _ in range(7)]
            + [pltpu.VMEM((seq, LANES), F32) for _ in range(4)]),
        compiler_params=pltpu.CompilerParams(
            dimension_semantics=("parallel", "parallel"), vmem_limit_bytes=VMEM_LIMIT),
        name="mlstm",
    )(proj, proj, proj, proj, proj, gate_bias_row, conv_w, conv_w, conv_b, conv_b, gain)


def _mix_out_body(x_ref, a_ref, m_ref, wa_ref, wm_ref, g_ref, o_ref):
    h = (jnp.dot(a_ref[...].astype(BF16), wa_ref[...], preferred_element_type=F32)
         + jnp.dot(m_ref[...].astype(BF16), wm_ref[...], preferred_element_type=F32))
    o_ref[...] = x_ref[...] + _rms(h) * g_ref[...]


def _mix_out(x, attn, ml, w_out, gain, layer):
    t = x.shape[0]
    return pl.pallas_call(
        _mix_out_body,
        grid=(t // MIX_TM,),
        in_specs=[
            pl.BlockSpec((MIX_TM, D_MODEL), lambda i: (i, 0)),
            pl.BlockSpec((MIX_TM, ATT_WIDTH), lambda i: (i, 0)),
            pl.BlockSpec((MIX_TM, MLSTM_WIDTH), lambda i: (i, 0)),
            pl.BlockSpec((None, None, ATT_WIDTH, D_MODEL), lambda i: (layer, 0, 0, 0)),
            pl.BlockSpec((None, None, MLSTM_WIDTH, D_MODEL), lambda i: (layer, 1, 0, 0)),
            pl.BlockSpec((1, D_MODEL), lambda i: (0, 0)),
        ],
        out_specs=pl.BlockSpec((MIX_TM, D_MODEL), lambda i: (i, 0)),
        out_shape=jax.ShapeDtypeStruct((t, D_MODEL), F32),
        compiler_params=pltpu.CompilerParams(
            dimension_semantics=("parallel",), vmem_limit_bytes=VMEM_LIMIT),
        name="mix_out",
    )(x, attn, ml, w_out, w_out, gain)


def _rope_tables(seq):
    half = ROPE_DIM // 2
    inv_freq = ROPE_THETA ** (-jnp.arange(half, dtype=F32) / half)
    ang = jnp.arange(seq, dtype=F32)[:, None] * inv_freq[None, :]
    cos, sin = jnp.cos(ang), jnp.sin(ang)
    pad = jnp.zeros((seq, ATT_HEAD_DIM - ROPE_DIM), F32)
    zero = jnp.zeros((seq, half), F32)
    cos_t = jnp.concatenate([cos, cos, pad + 1.0], axis=1)
    sa_t = jnp.concatenate([-sin, zero, pad], axis=1)
    sb_t = jnp.concatenate([zero, sin, pad], axis=1)
    reps = LANES // ATT_HEAD_DIM
    return tuple(jnp.tile(t, (1, reps)) for t in (cos_t, sa_t, sb_t))


def kernel(x, norm_gain, ffn_w_in, ffn_w_out, mix_w_in, conv_w, conv_b, gate_bias, mlstm_norm_gain, mix_w_out):
    batch, seq, _ = x.shape
    depth = norm_gain.shape[0]
    w_in = ffn_w_in.astype(BF16)
    w_out = ffn_w_out.astype(BF16)
    mw_in = jnp.pad(mix_w_in, ((0, 0), (0, 0), (0, IN_PAD - mix_w_in.shape[2]))).astype(BF16)
    mw_out = mix_w_out.astype(BF16)
    gb_rows = jnp.pad(gate_bias.reshape(depth, 1, N_GATE_COLS), ((0, 0), (0, 0), (0, LANES - N_GATE_COLS)))
    cos, sa, sb = _rope_tables(seq)

    xt = x.reshape(batch * seq, D_MODEL)
    for layer in range(depth):
        g = norm_gain[layer][:, None, :]
        xt = _ffn(xt, g[0], g[1], w_in, w_out, layer, 0)
        proj = _mix_in(xt, g[2], mw_in[layer], cos, sa, sb, seq).reshape(batch, seq, IN_PAD)
        attn = _attn(proj, batch, seq)
        ml = _mlstm(proj, gb_rows[layer], conv_w[layer], conv_b[layer][None, :],
                    mlstm_norm_gain[layer][None, :], batch, seq)
        xt = _mix_out(xt, attn.reshape(batch * seq, ATT_WIDTH), ml.reshape(batch * seq, MLSTM_WIDTH),
                      mw_out.reshape(depth, 2, ATT_WIDTH, D_MODEL), g[3], layer)
        xt = _ffn(xt, g[4], g[5], w_in, w_out, layer, 1)
    return xt.reshape(batch, seq, D_MODEL)
```

```python
import jax
import jax.numpy as jnp
from jax import lax
from jax.experimental import pallas as pl
from jax.experimental.pallas import tpu as pltpu

F32 = jnp.float32
BF16 = jnp.bfloat16

D_MODEL = 1024
ATT_HEAD_DIM = 64
ATT_WIDTH = 512
ATT_HEADS = 8
DILATIONS = (1, 4, 16)
N_SIDE = 64
ROPE_THETA = 500000.0
ROPE_DIM = 16
MLSTM_WIDTH = 512
MLSTM_HEADS = 4
MLSTM_HEAD_DIM = 128
MLSTM_CHUNK = 64
CONV_WIDTH = 5
N_GATE_COLS = 16
IN_MAIN = 3 * ATT_WIDTH + 4 * MLSTM_WIDTH
LANES = 128
IN_PAD = IN_MAIN + LANES
D_FF = 2816
NORM_EPS = 1e-6
NEG_BIG = -1e30

COL_AQ, COL_AK, COL_AV = 0, 4, 8
COL_MQ, COL_MK, COL_MV, COL_MO, COL_MG = 12, 16, 20, 24, 28

VMEM_LIMIT = 56 * 1024 * 1024

FFN_TM = 512
FFN_TF = 256
MIX_TM = 512
ATT_QB = 128
ATT_KB = ATT_QB + 2 * N_SIDE
ATT_REGROUP = 256
ATT_GROUP = 4
MLSTM_GROUP = 4
N_ROWS = 5
MLSTM_AUG = MLSTM_HEAD_DIM + 16
CONV_ROWS = 256
CONV_HALO = 8


def _rms(x):
    return x * lax.rsqrt(jnp.mean(x * x, axis=-1, keepdims=True) + NORM_EPS)


def _dot_nt(a, b):
    return lax.dot_general(a, b, (((1,), (1,)), ((), ())), preferred_element_type=F32)


def _ffn_body(x_ref, gpre_ref, gpost_ref, wg_ref, wu_ref, wo_ref, o_ref):
    x = x_ref[...]
    hn = (_rms(x) * gpre_ref[...]).astype(BF16)
    acc = None
    for c0 in range(0, D_FF, FFN_TF):
        g = jnp.dot(hn, wg_ref[:, c0:c0 + FFN_TF], preferred_element_type=F32)
        u = jnp.dot(hn, wu_ref[:, c0:c0 + FFN_TF], preferred_element_type=F32)
        a = (g * jax.nn.sigmoid(g) * u).astype(BF16)
        part = jnp.dot(a, wo_ref[c0:c0 + FFN_TF, :], preferred_element_type=F32)
        acc = part if acc is None else acc + part
    o_ref[...] = x + 0.5 * (_rms(acc) * gpost_ref[...])


def _ffn(x, gpre, gpost, w_in, w_out, layer, which):
    t = x.shape[0]
    resident = dict(pipeline_mode=pl.Buffered(1))
    return pl.pallas_call(
        _ffn_body,
        grid=(t // FFN_TM,),
        in_specs=[
            pl.BlockSpec((FFN_TM, D_MODEL), lambda i: (i, 0)),
            pl.BlockSpec((1, D_MODEL), lambda i: (0, 0)),
            pl.BlockSpec((1, D_MODEL), lambda i: (0, 0)),
            pl.BlockSpec((None, None, D_MODEL, D_FF), lambda i: (layer, which, 0, 0), **resident),
            pl.BlockSpec((None, None, D_MODEL, D_FF), lambda i: (layer, which, 0, 1), **resident),
            pl.BlockSpec((None, None, D_FF, D_MODEL), lambda i: (layer, which, 0, 0), **resident),
        ],
        out_specs=pl.BlockSpec((FFN_TM, D_MODEL), lambda i: (i, 0)),
        out_shape=jax.ShapeDtypeStruct((t, D_MODEL), F32),
        compiler_params=pltpu.CompilerParams(
            dimension_semantics=("parallel",), vmem_limit_bytes=VMEM_LIMIT),
        name="ffn",
    )(x, gpre, gpost, w_in, w_in, w_out)


def _mix_in_body(x_ref, g_ref, w_ref, cos_ref, sa_ref, sb_ref, o_ref):
    hn = (_rms(x_ref[...]) * g_ref[...]).astype(BF16)
    cos, sa, sb = cos_ref[...], sa_ref[...], sb_ref[...]

    def rotary(t):
        return t * cos + pltpu.roll(t, LANES - ROPE_DIM // 2, 1) * sa + pltpu.roll(t, ROPE_DIM // 2, 1) * sb

    for c0 in range(0, IN_PAD, ATT_WIDTH):
        c1 = min(c0 + ATT_WIDTH, IN_PAD)
        res = jnp.dot(hn, w_ref[:, c0:c1], preferred_element_type=F32)
        if c0 < 2 * ATT_WIDTH:
            scale = ATT_HEAD_DIM ** -0.5 if c0 == 0 else 1.0
            for s in range(0, ATT_WIDTH, LANES):
                o_ref[:, c0 + s:c0 + s + LANES] = rotary(res[:, s:s + LANES]) * scale
        else:
            o_ref[:, c0:c1] = res


def _mix_in(x, gain, w, cos, sa, sb, seq):
    t = x.shape[0]
    nseq = seq // MIX_TM
    tab = pl.BlockSpec((MIX_TM, LANES), lambda i: (i % nseq, 0))
    return pl.pallas_call(
        _mix_in_body,
        grid=(t // MIX_TM,),
        in_specs=[
            pl.BlockSpec((MIX_TM, D_MODEL), lambda i: (i, 0)),
            pl.BlockSpec((1, D_MODEL), lambda i: (0, 0)),
            pl.BlockSpec((D_MODEL, IN_PAD), lambda i: (0, 0)),
            tab, tab, tab,
        ],
        out_specs=pl.BlockSpec((MIX_TM, IN_PAD), lambda i: (i, 0)),
        out_shape=jax.ShapeDtypeStruct((t, IN_PAD), F32),
        compiler_params=pltpu.CompilerParams(
            dimension_semantics=("parallel",), vmem_limit_bytes=VMEM_LIMIT),
        name="mix_in",
    )(x, gain, w, cos, sa, sb)


def _attn_body(q_ref, k_ref, v_ref, o_ref, qd, kl, kh, vl, vh, bias, *scr):
    seq = q_ref.shape[0]
    o_scr, l_scr = scr[:3], scr[3:]
    low = lax.broadcasted_iota(jnp.int32, (1, LANES), 1) < ATT_HEAD_DIM
    diff = (lax.broadcasted_iota(jnp.int32, (ATT_QB, ATT_KB), 0)
            - lax.broadcasted_iota(jnp.int32, (ATT_QB, ATT_KB), 1))
    for i in range(3):
        off = diff + i * N_SIDE
        bias[i] = jnp.where((off >= -N_SIDE) & (off <= N_SIDE), 0.0, NEG_BIG)

    for gi, d in enumerate(DILATIONS):
        n = seq // d
        nblk = n // ATT_QB
        nreg = n // ATT_REGROUP

        def rows(start, size, d=d):
            return pl.ds(start, size) if d == 1 else pl.ds(start, size, stride=d)

        def regroup(t, carry, d=d, nreg=nreg, rows=rows):
            src = rows(t // nreg + (t % nreg) * (ATT_REGROUP * d), ATT_REGROUP)
            dst = pl.ds(pl.multiple_of(t * ATT_REGROUP, ATT_REGROUP), ATT_REGROUP)
            qd[dst, :] = q_ref[src, :].astype(BF16)
            k = k_ref[src, :]
            kl[dst, :] = jnp.where(low, k, 0.0).astype(BF16)
            kh[dst, :] = jnp.where(low, 0.0, k).astype(BF16)
            v = v_ref[src, :]
            vl[dst, :] = jnp.where(low, v, 0.0).astype(BF16)
            vh[dst, :] = jnp.where(low, 0.0, v).astype(BF16)
            return carry

        lax.fori_loop(0, seq // ATT_REGROUP, regroup, 0)

        def body(t, carry, d=d, n=n, nblk=nblk, gi=gi, rows=rows):
            blocks = []
            for g in range(ATT_GROUP):
                tg = t * ATT_GROUP + g
                r = tg // nblk
                m0 = (tg % nblk) * ATT_QB
                ks = jnp.clip(m0 - N_SIDE, 0, n - ATT_KB)
                qrow = pl.ds(pl.multiple_of(r * n + m0, ATT_QB), ATT_QB)
                krow = pl.ds(pl.multiple_of(r * n + ks, N_SIDE), ATT_KB)
                blocks.append((r, m0, ks, qrow, krow))
            scores = []
            for r, m0, ks, qrow, krow in blocks:
                kbd = jnp.concatenate([kl[krow, :], kh[krow, :]], axis=0)
                scores.append(_dot_nt(qd[qrow, :], kbd))
            probs = []
            for (r, m0, ks, qrow, krow), s in zip(blocks, scores):
                mask = bias[(m0 - ks) // N_SIDE]
                s0 = s[:, :ATT_KB] + mask
                s1 = s[:, ATT_KB:] + mask
                mx0 = jnp.max(s0, axis=1, keepdims=True)
                mx1 = jnp.max(s1, axis=1, keepdims=True)
                p0 = jnp.exp(s0 - mx0)
                p1 = jnp.exp(s1 - mx1)
                l0 = jnp.sum(p0, axis=1, keepdims=True)
                l1 = jnp.sum(p1, axis=1, keepdims=True)
                probs.append((jnp.concatenate([p0, p1], axis=1).astype(BF16), mx0, mx1, l0, l1))
            outs = []
            for (r, m0, ks, qrow, krow), (p, mx0, mx1, l0, l1) in zip(blocks, probs):
                vbd = jnp.concatenate([vl[krow, :], vh[krow, :]], axis=0)
                outs.append(jnp.dot(p, vbd, preferred_element_type=F32))
            for (r, m0, ks, qrow, krow), (p, mx0, mx1, l0, l1), o in zip(blocks, probs, outs):
                dst = rows(r + m0 * d, ATT_QB)
                o_scr[gi][dst, :] = o * jnp.where(low, 1.0 / l0, 1.0 / l1)
                l_scr[gi][dst, :] = jnp.where(low, mx0 + jnp.log(l0), mx1 + jnp.log(l1))
            return carry

        lax.fori_loop(0, d * nblk // ATT_GROUP, body, 0)

    def merge(c, carry):
        rs = pl.ds(pl.multiple_of(c * ATT_QB, ATT_QB), ATT_QB)
        ls = [l_scr[g][rs, :] for g in range(3)]
        mx = jnp.maximum(jnp.maximum(ls[0], ls[1]), ls[2])
        ws = [jnp.exp(l - mx) for l in ls]
        den = ws[0] + ws[1] + ws[2]
        acc = ws[0] * o_scr[0][rs, :] + ws[1] * o_scr[1][rs, :] + ws[2] * o_scr[2][rs, :]
        o_ref[rs, :] = acc / den
        return carry

    lax.fori_loop(0, seq // ATT_QB, merge, 0)


def _attn(proj, batch, seq):
    blk = lambda col: pl.BlockSpec((None, seq, LANES), lambda b, p: (b, 0, col + p))
    return pl.pallas_call(
        _attn_body,
        grid=(batch, ATT_HEADS // 2),
        in_specs=[blk(COL_AQ), blk(COL_AK), blk(COL_AV)],
        out_specs=pl.BlockSpec((None, seq, LANES), lambda b, p: (b, 0, p)),
        out_shape=jax.ShapeDtypeStruct((batch, seq, ATT_WIDTH), F32),
        scratch_shapes=([pltpu.VMEM((seq, LANES), BF16) for _ in range(5)]
                        + [pltpu.VMEM((3, ATT_QB, ATT_KB), F32)]
                        + [pltpu.VMEM((seq, LANES), F32) for _ in range(6)]),
        compiler_params=pltpu.CompilerParams(
            dimension_semantics=("parallel", "parallel"), vmem_limit_bytes=VMEM_LIMIT),
        name="attn",
    )(proj, proj, proj)


def _log_sigmoid(x):
    return jnp.minimum(x, 0.0) - jnp.log1p(jnp.exp(-jnp.abs(x)))


def _mlstm_body(q_ref, k_ref, v_ref, og_ref, g_ref, gb_ref, cwq_ref, cwk_ref, cbq_ref, cbk_ref,
                gain_ref, o_ref, pad_scr, qs, ks, vt, y_in, y_out, rows_scr, cols_scr, h_f, h_b):
    seq = q_ref.shape[0]
    L = MLSTM_CHUNK
    nc = seq // L
    head = pl.program_id(1)

    def conv(src_ref, w_ref, b_ref, dst_ref, scale):
        pad_scr[0:CONV_HALO, :] = jnp.zeros((CONV_HALO, LANES), F32)
        pad_scr[CONV_HALO + seq:CONV_HALO + seq + CONV_HALO, :] = jnp.zeros((CONV_HALO, LANES), F32)
        pad_scr[CONV_HALO:CONV_HALO + seq, :] = src_ref[...]
        w = w_ref[...]
        b = b_ref[...]

        def chunk(c, carry):
            start = pl.multiple_of(c * CONV_ROWS, CONV_ROWS)
            acc = b
            for j in range(CONV_WIDTH):
                off = CONV_HALO - CONV_WIDTH // 2 + j
                acc = acc + w[j:j + 1, :] * pad_scr[pl.ds(start + off, CONV_ROWS), :]
            y = acc * jax.nn.sigmoid(acc) * scale
            dst_ref[pl.ds(start, CONV_ROWS), :] = y.astype(BF16)
            return carry

        lax.fori_loop(0, seq // CONV_ROWS, chunk, 0)

    conv(q_ref, cwq_ref, cbq_ref, qs, 1.0)
    conv(k_ref, cwk_ref, cbk_ref, ks, MLSTM_HEAD_DIM ** -0.5)

    aug_rows = lax.broadcasted_iota(jnp.int32, (MLSTM_AUG - MLSTM_HEAD_DIM, L), 0)
    aug_tile = jnp.where(aug_rows == 0, 1.0, 0.0).astype(BF16)

    def v_chunk(c, carry):
        rs = pl.ds(pl.multiple_of(c * L, L), L)
        vt[c, 0:MLSTM_HEAD_DIM, :] = v_ref[rs, :].T.astype(BF16)
        vt[c, MLSTM_HEAD_DIM:MLSTM_AUG, :] = aug_tile
        return carry

    lax.fori_loop(0, nc, v_chunk, 0, unroll=4)

    nrow = seq // LANES

    def gather_rows(j, carry):
        rs = pl.ds(pl.multiple_of(j * LANES, LANES), LANES)
        gates = g_ref[rs, 0:N_GATE_COLS] + gb_ref[:, 0:N_GATE_COLS]
        y_in[pl.ds(pl.multiple_of(j * N_GATE_COLS, N_GATE_COLS), N_GATE_COLS), :] = gates.T
        return carry

    lax.fori_loop(0, nrow, gather_rows, 0, unroll=4)

    def gate_row(q):
        return y_in[pl.ds(q * MLSTM_HEADS + head, nrow, stride=N_GATE_COLS), :]

    lane_c = lax.broadcasted_iota(jnp.int32, (nrow, LANES), 1)
    row_c = lax.broadcasted_iota(jnp.int32, (nrow, LANES), 0)
    pos = lane_c & (L - 1)
    upper = lane_c >= L
    chunk_id = 2 * row_c + jnp.where(upper, 1, 0)

    def seg_scan(x, op, ident, forward):
        sh = 1
        while sh < L:
            if forward:
                x = op(x, jnp.where(pos >= sh, pltpu.roll(x, sh, 1), ident))
            else:
                x = op(x, jnp.where(pos < L - sh, pltpu.roll(x, LANES - sh, 1), ident))
            sh *= 2
        return x

    def seg_all(x, op):
        sh = 1
        while sh < L:
            x = op(x, jnp.where((pos & sh) == 0, pltpu.roll(x, LANES - sh, 1), pltpu.roll(x, sh, 1)))
            sh *= 2
        return x

    def chunk_shift(x, k, forward):
        if k == 1:
            r = pltpu.roll(x, L, 1)
            if forward:
                return jnp.where(upper, r, pltpu.roll(r, 1, 0))
            return jnp.where(upper, pltpu.roll(r, nrow - 1, 0), r)
        return pltpu.roll(x, k // 2, 0) if forward else pltpu.roll(x, nrow - k // 2, 0)

    def gate_direction(forward, base):
        i_pre = gate_row(base)
        log_f = _log_sigmoid(gate_row(base + 1))
        cum = seg_scan(log_f, jnp.add, 0.0, forward)
        b_src = i_pre - cum
        b_max = seg_scan(b_src, jnp.maximum, -jnp.inf, forward)
        edge = (L - 1) if forward else 0
        total = seg_all(jnp.where(pos == edge, cum, 0.0), jnp.add)
        src = total - cum + i_pre
        src_max = seg_all(src, jnp.maximum)
        acc_a, acc_b = total, src_max
        k = 1
        while k < nc:
            valid = (chunk_id >= k) if forward else (chunk_id <= nc - 1 - k)
            prev_a, prev_b = chunk_shift(acc_a, k, forward), chunk_shift(acc_b, k, forward)
            acc_b = jnp.where(valid, jnp.maximum(prev_b + acc_a, acc_b), acc_b)
            acc_a = jnp.where(valid, acc_a + prev_a, acc_a)
            k *= 2
        m_new = jnp.maximum(acc_a + NEG_BIG, acc_b)
        first = (chunk_id == 0) if forward else (chunk_id == nc - 1)
        m_prev = jnp.where(first, NEG_BIG, chunk_shift(m_new, 1, forward))
        m_rel = jnp.maximum(m_prev, b_max)
        o = 0 if forward else N_ROWS
        rows_scr[o + 0] = -m_rel
        rows_scr[o + 1] = jnp.exp(src - m_new)
        rows_scr[o + 2] = jnp.exp(m_prev - m_rel)
        rows_scr[o + 3] = jnp.exp(-(cum + m_rel))
        rows_scr[o + 4] = jnp.exp(total + m_prev - m_new)
        y_out[pl.ds(0 if forward else 1, nrow, stride=N_GATE_COLS), :] = b_src

    y_out[...] = jnp.zeros_like(y_out)
    gate_direction(True, 0)
    gate_direction(False, 2)

    def scatter_cols(j, carry):
        tile = y_out[pl.ds(pl.multiple_of(j * N_GATE_COLS, N_GATE_COLS), N_GATE_COLS), :]
        cols_scr[pl.ds(pl.multiple_of(j * LANES, LANES), LANES), 0:N_GATE_COLS] = tile.T
        return carry

    lax.fori_loop(0, nrow, scatter_cols, 0, unroll=4)

    s_idx = lax.broadcasted_iota(jnp.int32, (L, L), 0)
    t_idx = lax.broadcasted_iota(jnp.int32, (L, L), 1)

    def both(it, carry):
        states = list(carry)
        half = MLSTM_GROUP // 2
        tasks = []
        for g in range(MLSTM_GROUP):
            tasks.append((0, it * MLSTM_GROUP + g, it * half, g // 2, g % 2, s_idx <= t_idx, h_f))
        for g in range(MLSTM_GROUP):
            tasks.append((1, nc - 1 - (it * MLSTM_GROUP + g), nrow - (it + 1) * half,
                          half - 1 - g // 2, 1 - g % 2, s_idx >= t_idx, h_b))
        work = []
        for dirn, c, row0, r, hf, mask, h_ref in tasks:
            rs = pl.ds(pl.multiple_of(c * L, L), L)
            rows = [rows_scr[dirn * N_ROWS + i, pl.ds(row0, half), :][r:r + 1, hf * L:(hf + 1) * L]
                    for i in range(N_ROWS)]
            a_row, w_src, w_inter, e_neg, carry_scale = rows
            q, k, vta = qs[rs, :], ks[rs, :], vt[c]
            kq = _dot_nt(k, q)
            upd = jnp.dot((vta.astype(F32) * w_src).astype(BF16), k, preferred_element_type=F32)
            work.append(dict(dirn=dirn, rs=rs, q=q, vta=vta, kq=kq, upd=upd, mask=mask, h_ref=h_ref,
                             a_row=a_row, w_inter=w_inter, e_neg=e_neg, carry_scale=carry_scale[:, 0:1],
                             b_col=cols_scr[rs, dirn:dirn + 1]))
        for w in work:
            w["c_aug"] = states[w["dirn"]]
            states[w["dirn"]] = w["carry_scale"] * w["c_aug"] + w["upd"]
        for w in work:
            w["cq"] = _dot_nt(w["c_aug"].astype(BF16), w["q"])
        for w in work:
            decay = jnp.where(w["mask"], w["b_col"] + w["a_row"], -jnp.inf)
            w["st"] = (w["kq"] * jnp.exp(decay)).astype(BF16)
        for w in work:
            w["sv"] = jnp.dot(w["vta"], w["st"], preferred_element_type=F32)
        for w in work:
            num = w["sv"] + w["w_inter"] * w["cq"]
            den = num[MLSTM_HEAD_DIM:MLSTM_HEAD_DIM + 1, :]
            h_t = num[0:MLSTM_HEAD_DIM, :] * (1.0 / jnp.maximum(jnp.abs(den), w["e_neg"]))
            w["h_ref"][w["rs"], :] = h_t.T
        return tuple(states)

    init = jnp.zeros((MLSTM_AUG, MLSTM_HEAD_DIM), F32)
    lax.fori_loop(0, nc // MLSTM_GROUP, both, (init, init))

    def finish(c, carry):
        rs = pl.ds(pl.multiple_of(c * CONV_ROWS, CONV_ROWS), CONV_ROWS)
        cell = h_f[rs, :] + h_b[rs, :]
        o_ref[rs, :] = jax.nn.sigmoid(og_ref[rs, :]) * (_rms(cell) * gain_ref[...])
        return carry

    lax.fori_loop(0, seq // CONV_ROWS, finish, 0)


def _mlstm(proj, gate_bias_row, conv_w, conv_b, gain, batch, seq):
    blk = lambda col: pl.BlockSpec((None, seq, LANES), lambda b, h: (b, 0, col + h))
    row = lambda off: pl.BlockSpec((1, LANES), lambda b, h: (0, off + h))
    tap = lambda off: pl.BlockSpec((CONV_WIDTH, LANES), lambda b, h: (0, off + h))
    return pl.pallas_call(
        _mlstm_body,
        grid=(batch, MLSTM_HEADS),
        in_specs=[
            blk(COL_MQ), blk(COL_MK), blk(COL_MV), blk(COL_MO),
            pl.BlockSpec((None, seq, LANES), lambda b, h: (b, 0, COL_MG)),
            pl.BlockSpec((1, LANES), lambda b, h: (0, 0)),
            tap(0), tap(MLSTM_HEADS), row(0), row(MLSTM_HEADS), row(0),
        ],
        out_specs=pl.BlockSpec((None, seq, LANES), lambda b, h: (b, 0, h)),
        out_shape=jax.ShapeDtypeStruct((batch, seq, MLSTM_WIDTH), F32),
        scratch_shapes=(
            [pltpu.VMEM((seq + 2 * CONV_HALO, LANES), F32)]
            + [pltpu.VMEM((seq, LANES), BF16) for _ in range(2)]
            + [pltpu.VMEM((seq // MLSTM_CHUNK, MLSTM_AUG, MLSTM_CHUNK), BF16),
               pltpu.VMEM((seq // LANES * N_GATE_COLS, LANES), F32),
               pltpu.VMEM((seq // LANES * N_GATE_COLS, LANES), F32),
               pltpu.VMEM((2 * N_ROWS, seq // LANES, LANES), F32)]
            + [pltpu.VMEM((seq, LANES), F32) for _ in range(3)]),
        compiler_params=pltpu.CompilerParams(
            dimension_semantics=("parallel", "parallel"), vmem_limit_bytes=VMEM_LIMIT),
        name="mlstm",
    )(proj, proj, proj, proj, proj, gate_bias_row, conv_w, conv_w, conv_b, conv_b, gain)


def _mix_out_body(x_ref, a_ref, m_ref, wa_ref, wm_ref, g_ref, o_ref):
    h = (jnp.dot(a_ref[...].astype(BF16), wa_ref[...], preferred_element_type=F32)
         + jnp.dot(m_ref[...].astype(BF16), wm_ref[...], preferred_element_type=F32))
    o_ref[...] = x_ref[...] + _rms(h) * g_ref[...]


def _mix_out(x, attn, ml, w_out, gain, layer):
    t = x.shape[0]
    return pl.pallas_call(
        _mix_out_body,
        grid=(t // MIX_TM,),
        in_specs=[
            pl.BlockSpec((MIX_TM, D_MODEL), lambda i: (i, 0)),
            pl.BlockSpec((MIX_TM, ATT_WIDTH), lambda i: (i, 0)),
            pl.BlockSpec((MIX_TM, MLSTM_WIDTH), lambda i: (i, 0)),
            pl.BlockSpec((None, None, ATT_WIDTH, D_MODEL), lambda i: (layer, 0, 0, 0)),
            pl.BlockSpec((None, None, MLSTM_WIDTH, D_MODEL), lambda i: (layer, 1, 0, 0)),
            pl.BlockSpec((1, D_MODEL), lambda i: (0, 0)),
        ],
        out_specs=pl.BlockSpec((MIX_TM, D_MODEL), lambda i: (i, 0)),
        out_shape=jax.ShapeDtypeStruct((t, D_MODEL), F32),
        compiler_params=pltpu.CompilerParams(
            dimension_semantics=("parallel",), vmem_limit_bytes=VMEM_LIMIT),
        name="mix_out",
    )(x, attn, ml, w_out, w_out, gain)


def _rope_tables(seq):
    half = ROPE_DIM // 2
    inv_freq = ROPE_THETA ** (-jnp.arange(half, dtype=F32) / half)
    ang = jnp.arange(seq, dtype=F32)[:, None] * inv_freq[None, :]
    cos, sin = jnp.cos(ang), jnp.sin(ang)
    pad = jnp.zeros((seq, ATT_HEAD_DIM - ROPE_DIM), F32)
    zero = jnp.zeros((seq, half), F32)
    cos_t = jnp.concatenate([cos, cos, pad + 1.0], axis=1)
    sa_t = jnp.concatenate([-sin, zero, pad], axis=1)
    sb_t = jnp.concatenate([zero, sin, pad], axis=1)
    reps = LANES // ATT_HEAD_DIM
    return tuple(jnp.tile(t, (1, reps)) for t in (cos_t, sa_t, sb_t))


def kernel(x, norm_gain, ffn_w_in, ffn_w_out, mix_w_in, conv_w, conv_b, gate_bias, mlstm_norm_gain, mix_w_out):
    batch, seq, _ = x.shape
    depth = norm_gain.shape[0]
    w_in = ffn_w_in.astype(BF16)
    w_out = ffn_w_out.astype(BF16)
    mw_in = jnp.pad(mix_w_in, ((0, 0), (0, 0), (0, IN_PAD - mix_w_in.shape[2]))).astype(BF16)
    mw_out = mix_w_out.astype(BF16)
    gb_rows = jnp.pad(gate_bias.reshape(depth, 1, N_GATE_COLS), ((0, 0), (0, 0), (0, LANES - N_GATE_COLS)))
    cos, sa, sb = _rope_tables(seq)

    xt = x.reshape(batch * seq, D_MODEL)
    for layer in range(depth):
        g = norm_gain[layer][:, None, :]
        xt = _ffn(xt, g[0], g[1], w_in, w_out, layer, 0)
        proj = _mix_in(xt, g[2], mw_in[layer], cos, sa, sb, seq).reshape(batch, seq, IN_PAD)
        attn = _attn(proj, batch, seq)
        ml = _mlstm(proj, gb_rows[layer], conv_w[layer], conv_b[layer][None, :],
                    mlstm_norm_gain[layer][None, :], batch, seq)
        xt = _mix_out(xt, attn.reshape(batch * seq, ATT_WIDTH), ml.reshape(batch * seq, MLSTM_WIDTH),
                      mw_out.reshape(depth, 2, ATT_WIDTH, D_MODEL), g[3], layer)
        xt = _ffn(xt, g[4], g[5], w_in, w_out, layer, 1)
    return xt.reshape(batch, seq, D_MODEL)
```

```python
import jax
import jax.numpy as jnp
from jax import lax
from jax.experimental import pallas as pl
from jax.experimental.pallas import tpu as pltpu

F32 = jnp.float32
BF16 = jnp.bfloat16

D_MODEL = 1024
ATT_HEAD_DIM = 64
ATT_WIDTH = 512
ATT_HEADS = 8
DILATIONS = (1, 4, 16)
N_SIDE = 64
ROPE_THETA = 500000.0
ROPE_DIM = 16
MLSTM_WIDTH = 512
MLSTM_HEADS = 4
MLSTM_HEAD_DIM = 128
MLSTM_CHUNK = 64
CONV_WIDTH = 5
N_GATE_COLS = 16
IN_MAIN = 3 * ATT_WIDTH + 4 * MLSTM_WIDTH
LANES = 128
IN_PAD = IN_MAIN + LANES
D_FF = 2816
NORM_EPS = 1e-6
NEG_BIG = -1e30

COL_AQ, COL_AK, COL_AV = 0, 4, 8
COL_MQ, COL_MK, COL_MV, COL_MO, COL_MG = 12, 16, 20, 24, 28

VMEM_LIMIT = 56 * 1024 * 1024

FFN_TM = 512
FFN_TF = 256
MIX_TM = 512
ATT_QB = 128
ATT_KB = ATT_QB + 2 * N_SIDE
ATT_REGROUP = 256
ATT_GROUP = 4
MLSTM_GROUP = 4
N_ROWS = 5
MLSTM_AUG = MLSTM_HEAD_DIM + 16
CONV_ROWS = 256
CONV_HALO = 8


def _rms(x):
    return x * lax.rsqrt(jnp.mean(x * x, axis=-1, keepdims=True) + NORM_EPS)


def _dot_nt(a, b):
    return lax.dot_general(a, b, (((1,), (1,)), ((), ())), preferred_element_type=F32)


def _ffn_core(x, gpre_ref, gpost_ref, wg_ref, wu_ref, wo_ref, o_ref):
    hn = (_rms(x) * gpre_ref[...]).astype(BF16)
    acc = None
    for c0 in range(0, D_FF, FFN_TF):
        g = jnp.dot(hn, wg_ref[:, c0:c0 + FFN_TF], preferred_element_type=F32)
        u = jnp.dot(hn, wu_ref[:, c0:c0 + FFN_TF], preferred_element_type=F32)
        a = (g * jax.nn.sigmoid(g) * u).astype(BF16)
        part = jnp.dot(a, wo_ref[c0:c0 + FFN_TF, :], preferred_element_type=F32)
        acc = part if acc is None else acc + part
    o_ref[...] = x + 0.5 * (_rms(acc) * gpost_ref[...])


def _ffn_body(x_ref, gpre_ref, gpost_ref, wg_ref, wu_ref, wo_ref, o_ref):
    _ffn_core(x_ref[...], gpre_ref, gpost_ref, wg_ref, wu_ref, wo_ref, o_ref)


def _mix_ffn_body(x_ref, a_ref, m_ref, wa_ref, wm_ref, gmix_ref, gpre_ref, gpost_ref,
                  wg_ref, wu_ref, wo_ref, o_ref):
    h = (jnp.dot(a_ref[...].astype(BF16), wa_ref[...], preferred_element_type=F32)
         + jnp.dot(m_ref[...].astype(BF16), wm_ref[...], preferred_element_type=F32))
    x = x_ref[...] + _rms(h) * gmix_ref[...]
    _ffn_core(x, gpre_ref, gpost_ref, wg_ref, wu_ref, wo_ref, o_ref)


def _mix_ffn(x, attn, ml, mw_out, gmix, gpre, gpost, w_in, w_out, layer):
    t = x.shape[0]
    resident = dict(pipeline_mode=pl.Buffered(1))
    vec = pl.BlockSpec((1, D_MODEL), lambda i: (0, 0))
    return pl.pallas_call(
        _mix_ffn_body,
        grid=(t // FFN_TM,),
        in_specs=[
            pl.BlockSpec((FFN_TM, D_MODEL), lambda i: (i, 0)),
            pl.BlockSpec((FFN_TM, ATT_WIDTH), lambda i: (i, 0)),
            pl.BlockSpec((FFN_TM, MLSTM_WIDTH), lambda i: (i, 0)),
            pl.BlockSpec((None, None, ATT_WIDTH, D_MODEL), lambda i: (layer, 0, 0, 0), **resident),
            pl.BlockSpec((None, None, MLSTM_WIDTH, D_MODEL), lambda i: (layer, 1, 0, 0), **resident),
            vec, vec, vec,
            pl.BlockSpec((None, None, D_MODEL, D_FF), lambda i: (layer, 1, 0, 0), **resident),
            pl.BlockSpec((None, None, D_MODEL, D_FF), lambda i: (layer, 1, 0, 1), **resident),
            pl.BlockSpec((None, None, D_FF, D_MODEL), lambda i: (layer, 1, 0, 0), **resident),
        ],
        out_specs=pl.BlockSpec((FFN_TM, D_MODEL), lambda i: (i, 0)),
        out_shape=jax.ShapeDtypeStruct((t, D_MODEL), F32),
        compiler_params=pltpu.CompilerParams(
            dimension_semantics=("parallel",), vmem_limit_bytes=VMEM_LIMIT),
        name="mix_ffn",
    )(x, attn, ml, mw_out, mw_out, gmix, gpre, gpost, w_in, w_in, w_out)


def _ffn(x, gpre, gpost, w_in, w_out, layer, which):
    t = x.shape[0]
    resident = dict(pipeline_mode=pl.Buffered(1))
    return pl.pallas_call(
        _ffn_body,
        grid=(t // FFN_TM,),
        in_specs=[
            pl.BlockSpec((FFN_TM, D_MODEL), lambda i: (i, 0)),
            pl.BlockSpec((1, D_MODEL), lambda i: (0, 0)),
            pl.BlockSpec((1, D_MODEL), lambda i: (0, 0)),
            pl.BlockSpec((None, None, D_MODEL, D_FF), lambda i: (layer, which, 0, 0), **resident),
            pl.BlockSpec((None, None, D_MODEL, D_FF), lambda i: (layer, which, 0, 1), **resident),
            pl.BlockSpec((None, None, D_FF, D_MODEL), lambda i: (layer, which, 0, 0), **resident),
        ],
        out_specs=pl.BlockSpec((FFN_TM, D_MODEL), lambda i: (i, 0)),
        out_shape=jax.ShapeDtypeStruct((t, D_MODEL), F32),
        compiler_params=pltpu.CompilerParams(
            dimension_semantics=("parallel",), vmem_limit_bytes=VMEM_LIMIT),
        name="ffn",
    )(x, gpre, gpost, w_in, w_in, w_out)


def _mix_in_body(x_ref, g_ref, w_ref, cos_ref, sa_ref, sb_ref, o_ref):
    hn = (_rms(x_ref[...]) * g_ref[...]).astype(BF16)
    cos, sa, sb = cos_ref[...], sa_ref[...], sb_ref[...]

    def rotary(t):
        return t * cos + pltpu.roll(t, LANES - ROPE_DIM // 2, 1) * sa + pltpu.roll(t, ROPE_DIM // 2, 1) * sb

    for c0 in range(0, IN_PAD, ATT_WIDTH):
        c1 = min(c0 + ATT_WIDTH, IN_PAD)
        res = jnp.dot(hn, w_ref[:, c0:c1], preferred_element_type=F32)
        if c0 < 2 * ATT_WIDTH:
            scale = ATT_HEAD_DIM ** -0.5 if c0 == 0 else 1.0
            for s in range(0, ATT_WIDTH, LANES):
                o_ref[:, c0 + s:c0 + s + LANES] = rotary(res[:, s:s + LANES]) * scale
        else:
            o_ref[:, c0:c1] = res


def _mix_in(x, gain, w, cos, sa, sb, seq):
    t = x.shape[0]
    nseq = seq // MIX_TM
    tab = pl.BlockSpec((MIX_TM, LANES), lambda i: (i % nseq, 0))
    return pl.pallas_call(
        _mix_in_body,
        grid=(t // MIX_TM,),
        in_specs=[
            pl.BlockSpec((MIX_TM, D_MODEL), lambda i: (i, 0)),
            pl.BlockSpec((1, D_MODEL), lambda i: (0, 0)),
            pl.BlockSpec((D_MODEL, IN_PAD), lambda i: (0, 0)),
            tab, tab, tab,
        ],
        out_specs=pl.BlockSpec((MIX_TM, IN_PAD), lambda i: (i, 0)),
        out_shape=jax.ShapeDtypeStruct((t, IN_PAD), F32),
        compiler_params=pltpu.CompilerParams(
            dimension_semantics=("parallel",), vmem_limit_bytes=VMEM_LIMIT),
        name="mix_in",
    )(x, gain, w, cos, sa, sb)


def _attn_body(q_ref, k_ref, v_ref, o_ref, qd, kl, kh, vl, vh, bias, *scr):
    seq = q_ref.shape[0]
    o_scr, l_scr = scr[:3], scr[3:]
    low = lax.broadcasted_iota(jnp.int32, (1, LANES), 1) < ATT_HEAD_DIM
    diff = (lax.broadcasted_iota(jnp.int32, (ATT_QB, ATT_KB), 0)
            - lax.broadcasted_iota(jnp.int32, (ATT_QB, ATT_KB), 1))
    for i in range(3):
        off = diff + i * N_SIDE
        bias[i] = jnp.where((off >= -N_SIDE) & (off <= N_SIDE), 0.0, NEG_BIG)

    for gi, d in enumerate(DILATIONS):
        n = seq // d
        nblk = n // ATT_QB
        nreg = n // ATT_REGROUP

        def rows(start, size, d=d):
            return pl.ds(start, size) if d == 1 else pl.ds(start, size, stride=d)

        def regroup(t, carry, d=d, nreg=nreg, rows=rows):
            src = rows(t // nreg + (t % nreg) * (ATT_REGROUP * d), ATT_REGROUP)
            dst = pl.ds(pl.multiple_of(t * ATT_REGROUP, ATT_REGROUP), ATT_REGROUP)
            qd[dst, :] = q_ref[src, :].astype(BF16)
            k = k_ref[src, :]
            kl[dst, :] = jnp.where(low, k, 0.0).astype(BF16)
            kh[dst, :] = jnp.where(low, 0.0, k).astype(BF16)
            v = v_ref[src, :]
            vl[dst, :] = jnp.where(low, v, 0.0).astype(BF16)
            vh[dst, :] = jnp.where(low, 0.0, v).astype(BF16)
            return carry

        lax.fori_loop(0, seq // ATT_REGROUP, regroup, 0)

        def body(t, carry, d=d, n=n, nblk=nblk, gi=gi, rows=rows):
            blocks = []
            for g in range(ATT_GROUP):
                tg = t * ATT_GROUP + g
                r = tg // nblk
                m0 = (tg % nblk) * ATT_QB
                ks = jnp.clip(m0 - N_SIDE, 0, n - ATT_KB)
                qrow = pl.ds(pl.multiple_of(r * n + m0, ATT_QB), ATT_QB)
                krow = pl.ds(pl.multiple_of(r * n + ks, N_SIDE), ATT_KB)
                blocks.append((r, m0, ks, qrow, krow))
            scores = []
            for r, m0, ks, qrow, krow in blocks:
                kbd = jnp.concatenate([kl[krow, :], kh[krow, :]], axis=0)
                scores.append(_dot_nt(qd[qrow, :], kbd))
            probs = []
            for (r, m0, ks, qrow, krow), s in zip(blocks, scores):
                mask = bias[(m0 - ks) // N_SIDE]
                s0 = s[:, :ATT_KB] + mask
                s1 = s[:, ATT_KB:] + mask
                mx0 = jnp.max(s0, axis=1, keepdims=True)
                mx1 = jnp.max(s1, axis=1, keepdims=True)
                p0 = jnp.exp(s0 - mx0)
                p1 = jnp.exp(s1 - mx1)
                l0 = jnp.sum(p0, axis=1, keepdims=True)
                l1 = jnp.sum(p1, axis=1, keepdims=True)
                probs.append((jnp.concatenate([p0, p1], axis=1).astype(BF16), mx0, mx1, l0, l1))
            outs = []
            for (r, m0, ks, qrow, krow), (p, mx0, mx1, l0, l1) in zip(blocks, probs):
                vbd = jnp.concatenate([vl[krow, :], vh[krow, :]], axis=0)
                outs.append(jnp.dot(p, vbd, preferred_element_type=F32))
            for (r, m0, ks, qrow, krow), (p, mx0, mx1, l0, l1), o in zip(blocks, probs, outs):
                dst = rows(r + m0 * d, ATT_QB)
                o_scr[gi][dst, :] = o * jnp.where(low, 1.0 / l0, 1.0 / l1)
                l_scr[gi][dst, :] = jnp.where(low, mx0 + jnp.log(l0), mx1 + jnp.log(l1))
            return carry

        lax.fori_loop(0, d * nblk // ATT_GROUP, body, 0)

    def merge(c, carry):
        rs = pl.ds(pl.multiple_of(c * ATT_QB, ATT_QB), ATT_QB)
        ls = [l_scr[g][rs, :] for g in range(3)]
        mx = jnp.maximum(jnp.maximum(ls[0], ls[1]), ls[2])
        ws = [jnp.exp(l - mx) for l in ls]
        den = ws[0] + ws[1] + ws[2]
        acc = ws[0] * o_scr[0][rs, :] + ws[1] * o_scr[1][rs, :] + ws[2] * o_scr[2][rs, :]
        o_ref[rs, :] = acc / den
        return carry

    lax.fori_loop(0, seq // ATT_QB, merge, 0)


def _attn(proj, batch, seq):
    blk = lambda col: pl.BlockSpec((None, seq, LANES), lambda b, p: (b, 0, col + p))
    return pl.pallas_call(
        _attn_body,
        grid=(batch, ATT_HEADS // 2),
        in_specs=[blk(COL_AQ), blk(COL_AK), blk(COL_AV)],
        out_specs=pl.BlockSpec((None, seq, LANES), lambda b, p: (b, 0, p)),
        out_shape=jax.ShapeDtypeStruct((batch, seq, ATT_WIDTH), F32),
        scratch_shapes=([pltpu.VMEM((seq, LANES), BF16) for _ in range(5)]
                        + [pltpu.VMEM((3, ATT_QB, ATT_KB), F32)]
                        + [pltpu.VMEM((seq, LANES), F32) for _ in range(6)]),
        compiler_params=pltpu.CompilerParams(
            dimension_semantics=("parallel", "parallel"), vmem_limit_bytes=VMEM_LIMIT),
        name="attn",
    )(proj, proj, proj)


def _log_sigmoid(x):
    return jnp.minimum(x, 0.0) - jnp.log1p(jnp.exp(-jnp.abs(x)))


def _mlstm_body(q_ref, k_ref, v_ref, og_ref, g_ref, gb_ref, cwq_ref, cwk_ref, cbq_ref, cbk_ref,
                gain_ref, o_ref, pad_scr, qs, ks, vt, y_in, y_out, rows_scr, cols_scr, h_f, h_b):
    seq = q_ref.shape[0]
    L = MLSTM_CHUNK
    nc = seq // L
    head = pl.program_id(1)

    def conv(src_ref, w_ref, b_ref, dst_ref, scale):
        pad_scr[0:CONV_HALO, :] = jnp.zeros((CONV_HALO, LANES), F32)
        pad_scr[CONV_HALO + seq:CONV_HALO + seq + CONV_HALO, :] = jnp.zeros((CONV_HALO, LANES), F32)
        pad_scr[CONV_HALO:CONV_HALO + seq, :] = src_ref[...]
        w = w_ref[...]
        b = b_ref[...]

        def chunk(c, carry):
            start = pl.multiple_of(c * CONV_ROWS, CONV_ROWS)
            acc = b
            for j in range(CONV_WIDTH):
                off = CONV_HALO - CONV_WIDTH // 2 + j
                acc = acc + w[j:j + 1, :] * pad_scr[pl.ds(start + off, CONV_ROWS), :]
            y = acc * jax.nn.sigmoid(acc) * scale
            dst_ref[pl.ds(start, CONV_ROWS), :] = y.astype(BF16)
            return carry

        lax.fori_loop(0, seq // CONV_ROWS, chunk, 0)

    conv(q_ref, cwq_ref, cbq_ref, qs, 1.0)
    conv(k_ref, cwk_ref, cbk_ref, ks, MLSTM_HEAD_DIM ** -0.5)

    aug_rows = lax.broadcasted_iota(jnp.int32, (MLSTM_AUG - MLSTM_HEAD_DIM, L), 0)
    aug_tile = jnp.where(aug_rows == 0, 1.0, 0.0).astype(BF16)

    def v_chunk(c, carry):
        rs = pl.ds(pl.multiple_of(c * L, L), L)
        vt[c, 0:MLSTM_HEAD_DIM, :] = v_ref[rs, :].T.astype(BF16)
        vt[c, MLSTM_HEAD_DIM:MLSTM_AUG, :] = aug_tile
        return carry

    lax.fori_loop(0, nc, v_chunk, 0, unroll=4)

    nrow = seq // LANES

    def gather_rows(j, carry):
        rs = pl.ds(pl.multiple_of(j * LANES, LANES), LANES)
        gates = g_ref[rs, 0:N_GATE_COLS] + gb_ref[:, 0:N_GATE_COLS]
        y_in[pl.ds(pl.multiple_of(j * N_GATE_COLS, N_GATE_COLS), N_GATE_COLS), :] = gates.T
        return carry

    lax.fori_loop(0, nrow, gather_rows, 0, unroll=4)

    def gate_row(q):
        return y_in[pl.ds(q * MLSTM_HEADS + head, nrow, stride=N_GATE_COLS), :]

    lane_c = lax.broadcasted_iota(jnp.int32, (nrow, LANES), 1)
    row_c = lax.broadcasted_iota(jnp.int32, (nrow, LANES), 0)
    pos = lane_c & (L - 1)
    upper = lane_c >= L
    chunk_id = 2 * row_c + jnp.where(upper, 1, 0)

    def seg_scan(x, op, ident, forward):
        sh = 1
        while sh < L:
            if forward:
                x = op(x, jnp.where(pos >= sh, pltpu.roll(x, sh, 1), ident))
            else:
                x = op(x, jnp.where(pos < L - sh, pltpu.roll(x, LANES - sh, 1), ident))
            sh *= 2
        return x

    def seg_all(x, op):
        sh = 1
        while sh < L:
            x = op(x, jnp.where((pos & sh) == 0, pltpu.roll(x, LANES - sh, 1), pltpu.roll(x, sh, 1)))
            sh *= 2
        return x

    def chunk_shift(x, k, forward):
        if k == 1:
            r = pltpu.roll(x, L, 1)
            if forward:
                return jnp.where(upper, r, pltpu.roll(r, 1, 0))
            return jnp.where(upper, pltpu.roll(r, nrow - 1, 0), r)
        return pltpu.roll(x, k // 2, 0) if forward else pltpu.roll(x, nrow - k // 2, 0)

    def gate_direction(forward, base):
        i_pre = gate_row(base)
        log_f = _log_sigmoid(gate_row(base + 1))
        cum = seg_scan(log_f, jnp.add, 0.0, forward)
        b_src = i_pre - cum
        b_max = seg_scan(b_src, jnp.maximum, -jnp.inf, forward)
        edge = (L - 1) if forward else 0
        total = seg_all(jnp.where(pos == edge, cum, 0.0), jnp.add)
        src = total - cum + i_pre
        src_max = seg_all(src, jnp.maximum)
        acc_a, acc_b = total, src_max
        k = 1
        while k < nc:
            valid = (chunk_id >= k) if forward else (chunk_id <= nc - 1 - k)
            prev_a, prev_b = chunk_shift(acc_a, k, forward), chunk_shift(acc_b, k, forward)
            acc_b = jnp.where(valid, jnp.maximum(prev_b + acc_a, acc_b), acc_b)
            acc_a = jnp.where(valid, acc_a + prev_a, acc_a)
            k *= 2
        m_new = jnp.maximum(acc_a + NEG_BIG, acc_b)
        first = (chunk_id == 0) if forward else (chunk_id == nc - 1)
        m_prev = jnp.where(first, NEG_BIG, chunk_shift(m_new, 1, forward))
        m_rel = jnp.maximum(m_prev, b_max)
        o = 0 if forward else N_ROWS
        rows_scr[o + 0] = -m_rel
        rows_scr[o + 1] = jnp.exp(src - m_new)
        rows_scr[o + 2] = jnp.exp(m_prev - m_rel)
        rows_scr[o + 3] = jnp.exp(-(cum + m_rel))
        rows_scr[o + 4] = jnp.exp(total + m_prev - m_new)
        y_out[pl.ds(0 if forward else 1, nrow, stride=N_GATE_COLS), :] = b_src

    y_out[...] = jnp.zeros_like(y_out)
    gate_direction(True, 0)
    gate_direction(False, 2)

    def scatter_cols(j, carry):
        tile = y_out[pl.ds(pl.multiple_of(j * N_GATE_COLS, N_GATE_COLS), N_GATE_COLS), :]
        cols_scr[pl.ds(pl.multiple_of(j * LANES, LANES), LANES), 0:N_GATE_COLS] = tile.T
        return carry

    lax.fori_loop(0, nrow, scatter_cols, 0, unroll=4)

    s_idx = lax.broadcasted_iota(jnp.int32, (L, L), 0)
    t_idx = lax.broadcasted_iota(jnp.int32, (L, L), 1)

    def both(it, carry):
        states = list(carry)
        half = MLSTM_GROUP // 2
        tasks = []
        for g in range(MLSTM_GROUP):
            tasks.append((0, it * MLSTM_GROUP + g, it * half, g // 2, g % 2, s_idx <= t_idx, h_f))
        for g in range(MLSTM_GROUP):
            tasks.append((1, nc - 1 - (it * MLSTM_GROUP + g), nrow - (it + 1) * half,
                          half - 1 - g // 2, 1 - g % 2, s_idx >= t_idx, h_b))
        work = []
        for dirn, c, row0, r, hf, mask, h_ref in tasks:
            rs = pl.ds(pl.multiple_of(c * L, L), L)
            rows = [rows_scr[dirn * N_ROWS + i, pl.ds(row0, half), :][r:r + 1, hf * L:(hf + 1) * L]
                    for i in range(N_ROWS)]
            a_row, w_src, w_inter, e_neg, carry_scale = rows
            q, k, vta = qs[rs, :], ks[rs, :], vt[c]
            kq = _dot_nt(k, q)
            upd = jnp.dot((vta.astype(F32) * w_src).astype(BF16), k, preferred_element_type=F32)
            work.append(dict(dirn=dirn, rs=rs, q=q, vta=vta, kq=kq, upd=upd, mask=mask, h_ref=h_ref,
                             a_row=a_row, w_inter=w_inter, e_neg=e_neg, carry_scale=carry_scale[:, 0:1],
                             b_col=cols_scr[rs, dirn:dirn + 1]))
        for w in work:
            w["c_aug"] = states[w["dirn"]]
            states[w["dirn"]] = w["carry_scale"] * w["c_aug"] + w["upd"]
        for w in work:
            w["cq"] = _dot_nt(w["c_aug"].astype(BF16), w["q"])
        for w in work:
            decay = jnp.where(w["mask"], w["b_col"] + w["a_row"], -jnp.inf)
            w["st"] = (w["kq"] * jnp.exp(decay)).astype(BF16)
        for w in work:
            w["sv"] = jnp.dot(w["vta"], w["st"], preferred_element_type=F32)
        for w in work:
            num = w["sv"] + w["w_inter"] * w["cq"]
            den = num[MLSTM_HEAD_DIM:MLSTM_HEAD_DIM + 1, :]
            h_t = num[0:MLSTM_HEAD_DIM, :] * (1.0 / jnp.maximum(jnp.abs(den), w["e_neg"]))
            w["h_ref"][w["rs"], :] = h_t.T
        return tuple(states)

    init = jnp.zeros((MLSTM_AUG, MLSTM_HEAD_DIM), F32)
    lax.fori_loop(0, nc // MLSTM_GROUP, both, (init, init))

    def finish(c, carry):
        rs = pl.ds(pl.multiple_of(c * CONV_ROWS, CONV_ROWS), CONV_ROWS)
        cell = h_f[rs, :] + h_b[rs, :]
        o_ref[rs, :] = jax.nn.sigmoid(og_ref[rs, :]) * (_rms(cell) * gain_ref[...])
        return carry

    lax.fori_loop(0, seq // CONV_ROWS, finish, 0)


def _mlstm(proj, gate_bias_row, conv_w, conv_b, gain, batch, seq):
    blk = lambda col: pl.BlockSpec((None, seq, LANES), lambda b, h: (b, 0, col + h))
    row = lambda off: pl.BlockSpec((1, LANES), lambda b, h: (0, off + h))
    tap = lambda off: pl.BlockSpec((CONV_WIDTH, LANES), lambda b, h: (0, off + h))
    return pl.pallas_call(
        _mlstm_body,
        grid=(batch, MLSTM_HEADS),
        in_specs=[
            blk(COL_MQ), blk(COL_MK), blk(COL_MV), blk(COL_MO),
            pl.BlockSpec((None, seq, LANES), lambda b, h: (b, 0, COL_MG)),
            pl.BlockSpec((1, LANES), lambda b, h: (0, 0)),
            tap(0), tap(MLSTM_HEADS), row(0), row(MLSTM_HEADS), row(0),
        ],
        out_specs=pl.BlockSpec((None, seq, LANES), lambda b, h: (b, 0, h)),
        out_shape=jax.ShapeDtypeStruct((batch, seq, MLSTM_WIDTH), F32),
        scratch_shapes=(
            [pltpu.VMEM((seq + 2 * CONV_HALO, LANES), F32)]
            + [pltpu.VMEM((seq, LANES), BF16) for _ in range(2)]
            + [pltpu.VMEM((seq // MLSTM_CHUNK, MLSTM_AUG, MLSTM_CHUNK), BF16),
               pltpu.VMEM((seq // LANES * N_GATE_COLS, LANES), F32),
               pltpu.VMEM((seq // LANES * N_GATE_COLS, LANES), F32),
               pltpu.VMEM((2 * N_ROWS, seq // LANES, LANES), F32)]
            + [pltpu.VMEM((seq, LANES), F32) for _ in range(3)]),
        compiler_params=pltpu.CompilerParams(
            dimension_semantics=("parallel", "parallel"), vmem_limit_bytes=VMEM_LIMIT),
        name="mlstm",
    )(proj, proj, proj, proj, proj, gate_bias_row, conv_w, conv_w, conv_b, conv_b, gain)


def _rope_tables(seq):
    half = ROPE_DIM // 2
    inv_freq = ROPE_THETA ** (-jnp.arange(half, dtype=F32) / half)
    ang = jnp.arange(seq, dtype=F32)[:, None] * inv_freq[None, :]
    cos, sin = jnp.cos(ang), jnp.sin(ang)
    pad = jnp.zeros((seq, ATT_HEAD_DIM - ROPE_DIM), F32)
    zero = jnp.zeros((seq, half), F32)
    cos_t = jnp.concatenate([cos, cos, pad + 1.0], axis=1)
    sa_t = jnp.concatenate([-sin, zero, pad], axis=1)
    sb_t = jnp.concatenate([zero, sin, pad], axis=1)
    reps = LANES // ATT_HEAD_DIM
    return tuple(jnp.tile(t, (1, reps)) for t in (cos_t, sa_t, sb_t))


def kernel(x, norm_gain, ffn_w_in, ffn_w_out, mix_w_in, conv_w, conv_b, gate_bias, mlstm_norm_gain, mix_w_out):
    batch, seq, _ = x.shape
    depth = norm_gain.shape[0]
    w_in = ffn_w_in.astype(BF16)
    w_out = ffn_w_out.astype(BF16)
    mw_in = jnp.pad(mix_w_in, ((0, 0), (0, 0), (0, IN_PAD - mix_w_in.shape[2]))).astype(BF16)
    mw_out = mix_w_out.astype(BF16)
    gb_rows = jnp.pad(gate_bias.reshape(depth, 1, N_GATE_COLS), ((0, 0), (0, 0), (0, LANES - N_GATE_COLS)))
    cos, sa, sb = _rope_tables(seq)

    xt = x.reshape(batch * seq, D_MODEL)
    for layer in range(depth):
        g = norm_gain[layer][:, None, :]
        xt = _ffn(xt, g[0], g[1], w_in, w_out, layer, 0)
        proj = _mix_in(xt, g[2], mw_in[layer], cos, sa, sb, seq).reshape(batch, seq, IN_PAD)
        attn = _attn(proj, batch, seq)
        ml = _mlstm(proj, gb_rows[layer], conv_w[layer], conv_b[layer][None, :],
                    mlstm_norm_gain[layer][None, :], batch, seq)
        xt = _mix_ffn(xt, attn.reshape(batch * seq, ATT_WIDTH), ml.reshape(batch * seq, MLSTM_WIDTH),
                      mw_out.reshape(depth, 2, ATT_WIDTH, D_MODEL), g[3], g[4], g[5], w_in, w_out, layer)
    return xt.reshape(batch, seq, D_MODEL)
```

```python
import jax
import jax.numpy as jnp
from jax import lax
from jax.experimental import pallas as pl
from jax.experimental.pallas import tpu as pltpu

F32 = jnp.float32
BF16 = jnp.bfloat16

D_MODEL = 1024
ATT_HEAD_DIM = 64
ATT_WIDTH = 512
ATT_HEADS = 8
DILATIONS = (1, 4, 16)
N_SIDE = 64
ROPE_THETA = 500000.0
ROPE_DIM = 16
MLSTM_WIDTH = 512
MLSTM_HEADS = 4
MLSTM_HEAD_DIM = 128
MLSTM_CHUNK = 64
CONV_WIDTH = 5
N_GATE_COLS = 16
IN_MAIN = 3 * ATT_WIDTH + 4 * MLSTM_WIDTH
LANES = 128
IN_PAD = IN_MAIN + LANES
D_FF = 2816
NORM_EPS = 1e-6
NEG_BIG = -1e30

COL_AQ, COL_AK, COL_AV = 0, 4, 8
COL_MQ, COL_MK, COL_MV, COL_MO, COL_MG = 12, 16, 20, 24, 28

VMEM_LIMIT = 56 * 1024 * 1024

FFN_TM = 512
FFN_TF = 256
MIX_TM = 512
ATT_QB = 128
ATT_KB = ATT_QB + 2 * N_SIDE
ATT_REGROUP = 256
ATT_GROUP = 4
MLSTM_GROUP = 8
N_ROWS = 5
MLSTM_AUG = MLSTM_HEAD_DIM + 16
CONV_ROWS = 256
CONV_HALO = 8


def _rms(x):
    return x * lax.rsqrt(jnp.mean(x * x, axis=-1, keepdims=True) + NORM_EPS)


def _dot_nt(a, b):
    return lax.dot_general(a, b, (((1,), (1,)), ((), ())), preferred_element_type=F32)


def _ffn_core(x, gpre_ref, gpost_ref, wg_ref, wu_ref, wo_ref, o_ref):
    hn = (_rms(x) * gpre_ref[...]).astype(BF16)
    acc = None
    for c0 in range(0, D_FF, FFN_TF):
        g = jnp.dot(hn, wg_ref[:, c0:c0 + FFN_TF], preferred_element_type=F32)
        u = jnp.dot(hn, wu_ref[:, c0:c0 + FFN_TF], preferred_element_type=F32)
        a = (g * jax.nn.sigmoid(g) * u).astype(BF16)
        part = jnp.dot(a, wo_ref[c0:c0 + FFN_TF, :], preferred_element_type=F32)
        acc = part if acc is None else acc + part
    o_ref[...] = x + 0.5 * (_rms(acc) * gpost_ref[...])


def _ffn_body(x_ref, gpre_ref, gpost_ref, wg_ref, wu_ref, wo_ref, o_ref):
    _ffn_core(x_ref[...], gpre_ref, gpost_ref, wg_ref, wu_ref, wo_ref, o_ref)


def _mix_ffn_body(x_ref, a_ref, m_ref, wa_ref, wm_ref, gmix_ref, gpre_ref, gpost_ref,
                  wg_ref, wu_ref, wo_ref, o_ref):
    h = (jnp.dot(a_ref[...].astype(BF16), wa_ref[...], preferred_element_type=F32)
         + jnp.dot(m_ref[...].astype(BF16), wm_ref[...], preferred_element_type=F32))
    x = x_ref[...] + _rms(h) * gmix_ref[...]
    _ffn_core(x, gpre_ref, gpost_ref, wg_ref, wu_ref, wo_ref, o_ref)


def _mix_ffn(x, attn, ml, mw_out, gmix, gpre, gpost, w_in, w_out, layer):
    t = x.shape[0]
    resident = dict(pipeline_mode=pl.Buffered(1))
    vec = pl.BlockSpec((1, D_MODEL), lambda i: (0, 0))
    return pl.pallas_call(
        _mix_ffn_body,
        grid=(t // FFN_TM,),
        in_specs=[
            pl.BlockSpec((FFN_TM, D_MODEL), lambda i: (i, 0)),
            pl.BlockSpec((FFN_TM, ATT_WIDTH), lambda i: (i, 0)),
            pl.BlockSpec((FFN_TM, MLSTM_WIDTH), lambda i: (i, 0)),
            pl.BlockSpec((None, None, ATT_WIDTH, D_MODEL), lambda i: (layer, 0, 0, 0), **resident),
            pl.BlockSpec((None, None, MLSTM_WIDTH, D_MODEL), lambda i: (layer, 1, 0, 0), **resident),
            vec, vec, vec,
            pl.BlockSpec((None, None, D_MODEL, D_FF), lambda i: (layer, 1, 0, 0), **resident),
            pl.BlockSpec((None, None, D_MODEL, D_FF), lambda i: (layer, 1, 0, 1), **resident),
            pl.BlockSpec((None, None, D_FF, D_MODEL), lambda i: (layer, 1, 0, 0), **resident),
        ],
        out_specs=pl.BlockSpec((FFN_TM, D_MODEL), lambda i: (i, 0)),
        out_shape=jax.ShapeDtypeStruct((t, D_MODEL), F32),
        compiler_params=pltpu.CompilerParams(
            dimension_semantics=("parallel",), vmem_limit_bytes=VMEM_LIMIT),
        name="mix_ffn",
    )(x, attn, ml, mw_out, mw_out, gmix, gpre, gpost, w_in, w_in, w_out)


def _ffn(x, gpre, gpost, w_in, w_out, layer, which):
    t = x.shape[0]
    resident = dict(pipeline_mode=pl.Buffered(1))
    return pl.pallas_call(
        _ffn_body,
        grid=(t // FFN_TM,),
        in_specs=[
            pl.BlockSpec((FFN_TM, D_MODEL), lambda i: (i, 0)),
            pl.BlockSpec((1, D_MODEL), lambda i: (0, 0)),
            pl.BlockSpec((1, D_MODEL), lambda i: (0, 0)),
            pl.BlockSpec((None, None, D_MODEL, D_FF), lambda i: (layer, which, 0, 0), **resident),
            pl.BlockSpec((None, None, D_MODEL, D_FF), lambda i: (layer, which, 0, 1), **resident),
            pl.BlockSpec((None, None, D_FF, D_MODEL), lambda i: (layer, which, 0, 0), **resident),
        ],
        out_specs=pl.BlockSpec((FFN_TM, D_MODEL), lambda i: (i, 0)),
        out_shape=jax.ShapeDtypeStruct((t, D_MODEL), F32),
        compiler_params=pltpu.CompilerParams(
            dimension_semantics=("parallel",), vmem_limit_bytes=VMEM_LIMIT),
        name="ffn",
    )(x, gpre, gpost, w_in, w_in, w_out)


def _mix_in_body(x_ref, g_ref, w_ref, cos_ref, sa_ref, sb_ref, o_ref):
    hn = (_rms(x_ref[...]) * g_ref[...]).astype(BF16)
    cos, sa, sb = cos_ref[...], sa_ref[...], sb_ref[...]

    def rotary(t):
        return t * cos + pltpu.roll(t, LANES - ROPE_DIM // 2, 1) * sa + pltpu.roll(t, ROPE_DIM // 2, 1) * sb

    for c0 in range(0, IN_PAD, ATT_WIDTH):
        c1 = min(c0 + ATT_WIDTH, IN_PAD)
        res = jnp.dot(hn, w_ref[:, c0:c1], preferred_element_type=F32)
        if c0 < 2 * ATT_WIDTH:
            scale = ATT_HEAD_DIM ** -0.5 if c0 == 0 else 1.0
            for s in range(0, ATT_WIDTH, LANES):
                o_ref[:, c0 + s:c0 + s + LANES] = rotary(res[:, s:s + LANES]) * scale
        else:
            o_ref[:, c0:c1] = res


def _mix_in(x, gain, w, cos, sa, sb, seq):
    t = x.shape[0]
    nseq = seq // MIX_TM
    tab = pl.BlockSpec((MIX_TM, LANES), lambda i: (i % nseq, 0))
    return pl.pallas_call(
        _mix_in_body,
        grid=(t // MIX_TM,),
        in_specs=[
            pl.BlockSpec((MIX_TM, D_MODEL), lambda i: (i, 0)),
            pl.BlockSpec((1, D_MODEL), lambda i: (0, 0)),
            pl.BlockSpec((D_MODEL, IN_PAD), lambda i: (0, 0)),
            tab, tab, tab,
        ],
        out_specs=pl.BlockSpec((MIX_TM, IN_PAD), lambda i: (i, 0)),
        out_shape=jax.ShapeDtypeStruct((t, IN_PAD), F32),
        compiler_params=pltpu.CompilerParams(
            dimension_semantics=("parallel",), vmem_limit_bytes=VMEM_LIMIT),
        name="mix_in",
    )(x, gain, w, cos, sa, sb)


def _attn_body(q_ref, k_ref, v_ref, o_ref, qd, kl, kh, vl, vh, bias, *scr):
    seq = q_ref.shape[0]
    o_scr, l_scr = scr[:3], scr[3:]
    low = lax.broadcasted_iota(jnp.int32, (1, LANES), 1) < ATT_HEAD_DIM
    diff = (lax.broadcasted_iota(jnp.int32, (ATT_QB, ATT_KB), 0)
            - lax.broadcasted_iota(jnp.int32, (ATT_QB, ATT_KB), 1))
    for i in range(3):
        off = diff + i * N_SIDE
        bias[i] = jnp.where((off >= -N_SIDE) & (off <= N_SIDE), 0.0, NEG_BIG)

    for gi, d in enumerate(DILATIONS):
        n = seq // d
        nblk = n // ATT_QB
        nreg = n // ATT_REGROUP

        def rows(start, size, d=d):
            return pl.ds(start, size) if d == 1 else pl.ds(start, size, stride=d)

        def regroup(t, carry, d=d, nreg=nreg, rows=rows):
            src = rows(t // nreg + (t % nreg) * (ATT_REGROUP * d), ATT_REGROUP)
            dst = pl.ds(pl.multiple_of(t * ATT_REGROUP, ATT_REGROUP), ATT_REGROUP)
            qd[dst, :] = q_ref[src, :].astype(BF16)
            k = k_ref[src, :]
            kl[dst, :] = jnp.where(low, k, 0.0).astype(BF16)
            kh[dst, :] = jnp.where(low, 0.0, k).astype(BF16)
            v = v_ref[src, :]
            vl[dst, :] = jnp.where(low, v, 0.0).astype(BF16)
            vh[dst, :] = jnp.where(low, 0.0, v).astype(BF16)
            return carry

        lax.fori_loop(0, seq // ATT_REGROUP, regroup, 0)

        def body(t, carry, d=d, n=n, nblk=nblk, gi=gi, rows=rows):
            blocks = []
            for g in range(ATT_GROUP):
                tg = t * ATT_GROUP + g
                r = tg // nblk
                m0 = (tg % nblk) * ATT_QB
                ks = jnp.clip(m0 - N_SIDE, 0, n - ATT_KB)
                qrow = pl.ds(pl.multiple_of(r * n + m0, ATT_QB), ATT_QB)
                krow = pl.ds(pl.multiple_of(r * n + ks, N_SIDE), ATT_KB)
                blocks.append((r, m0, ks, qrow, krow))
            scores = []
            for r, m0, ks, qrow, krow in blocks:
                kbd = jnp.concatenate([kl[krow, :], kh[krow, :]], axis=0)
                scores.append(_dot_nt(qd[qrow, :], kbd))
            probs = []
            for (r, m0, ks, qrow, krow), s in zip(blocks, scores):
                mask = bias[(m0 - ks) // N_SIDE]
                s0 = s[:, :ATT_KB] + mask
                s1 = s[:, ATT_KB:] + mask
                mx0 = jnp.max(s0, axis=1, keepdims=True)
                mx1 = jnp.max(s1, axis=1, keepdims=True)
                p0 = jnp.exp(s0 - mx0)
                p1 = jnp.exp(s1 - mx1)
                l0 = jnp.sum(p0, axis=1, keepdims=True)
                l1 = jnp.sum(p1, axis=1, keepdims=True)
                probs.append((jnp.concatenate([p0, p1], axis=1).astype(BF16), mx0, mx1, l0, l1))
            outs = []
            for (r, m0, ks, qrow, krow), (p, mx0, mx1, l0, l1) in zip(blocks, probs):
                vbd = jnp.concatenate([vl[krow, :], vh[krow, :]], axis=0)
                outs.append(jnp.dot(p, vbd, preferred_element_type=F32))
            for (r, m0, ks, qrow, krow), (p, mx0, mx1, l0, l1), o in zip(blocks, probs, outs):
                dst = rows(r + m0 * d, ATT_QB)
                o_scr[gi][dst, :] = o * jnp.where(low, 1.0 / l0, 1.0 / l1)
                l_scr[gi][dst, :] = jnp.where(low, mx0 + jnp.log(l0), mx1 + jnp.log(l1))
            return carry

        lax.fori_loop(0, d * nblk // ATT_GROUP, body, 0)

    def merge(c, carry):
        rs = pl.ds(pl.multiple_of(c * ATT_QB, ATT_QB), ATT_QB)
        ls = [l_scr[g][rs, :] for g in range(3)]
        mx = jnp.maximum(jnp.maximum(ls[0], ls[1]), ls[2])
        ws = [jnp.exp(l - mx) for l in ls]
        den = ws[0] + ws[1] + ws[2]
        acc = ws[0] * o_scr[0][rs, :] + ws[1] * o_scr[1][rs, :] + ws[2] * o_scr[2][rs, :]
        o_ref[rs, :] = acc / den
        return carry

    lax.fori_loop(0, seq // ATT_QB, merge, 0)


def _attn(proj, batch, seq):
    blk = lambda col: pl.BlockSpec((None, seq, LANES), lambda b, p: (b, 0, col + p))
    return pl.pallas_call(
        _attn_body,
        grid=(batch, ATT_HEADS // 2),
        in_specs=[blk(COL_AQ), blk(COL_AK), blk(COL_AV)],
        out_specs=pl.BlockSpec((None, seq, LANES), lambda b, p: (b, 0, p)),
        out_shape=jax.ShapeDtypeStruct((batch, seq, ATT_WIDTH), F32),
        scratch_shapes=([pltpu.VMEM((seq, LANES), BF16) for _ in range(5)]
                        + [pltpu.VMEM((3, ATT_QB, ATT_KB), F32)]
                        + [pltpu.VMEM((seq, LANES), F32) for _ in range(6)]),
        compiler_params=pltpu.CompilerParams(
            dimension_semantics=("parallel", "parallel"), vmem_limit_bytes=VMEM_LIMIT),
        name="attn",
    )(proj, proj, proj)


def _log_sigmoid(x):
    return jnp.minimum(x, 0.0) - jnp.log1p(jnp.exp(-jnp.abs(x)))


def _mlstm_body(q_ref, k_ref, v_ref, og_ref, g_ref, gb_ref, cwq_ref, cwk_ref, cbq_ref, cbk_ref,
                gain_ref, o_ref, pad_scr, qs, ks, vt, y_in, y_out, rows_scr, cols_scr, h_f, h_b):
    seq = q_ref.shape[0]
    L = MLSTM_CHUNK
    nc = seq // L
    head = pl.program_id(1)

    def conv(src_ref, w_ref, b_ref, dst_ref, scale):
        pad_scr[0:CONV_HALO, :] = jnp.zeros((CONV_HALO, LANES), F32)
        pad_scr[CONV_HALO + seq:CONV_HALO + seq + CONV_HALO, :] = jnp.zeros((CONV_HALO, LANES), F32)
        pad_scr[CONV_HALO:CONV_HALO + seq, :] = src_ref[...]
        w = w_ref[...]
        b = b_ref[...]

        def chunk(c, carry):
            start = pl.multiple_of(c * CONV_ROWS, CONV_ROWS)
            acc = b
            for j in range(CONV_WIDTH):
                off = CONV_HALO - CONV_WIDTH // 2 + j
                acc = acc + w[j:j + 1, :] * pad_scr[pl.ds(start + off, CONV_ROWS), :]
            y = acc * jax.nn.sigmoid(acc) * scale
            dst_ref[pl.ds(start, CONV_ROWS), :] = y.astype(BF16)
            return carry

        lax.fori_loop(0, seq // CONV_ROWS, chunk, 0)

    conv(q_ref, cwq_ref, cbq_ref, qs, 1.0)
    conv(k_ref, cwk_ref, cbk_ref, ks, MLSTM_HEAD_DIM ** -0.5)

    aug_rows = lax.broadcasted_iota(jnp.int32, (MLSTM_AUG - MLSTM_HEAD_DIM, L), 0)
    aug_tile = jnp.where(aug_rows == 0, 1.0, 0.0).astype(BF16)

    def v_chunk(c, carry):
        rs = pl.ds(pl.multiple_of(c * L, L), L)
        vt[c, 0:MLSTM_HEAD_DIM, :] = v_ref[rs, :].T.astype(BF16)
        vt[c, MLSTM_HEAD_DIM:MLSTM_AUG, :] = aug_tile
        return carry

    lax.fori_loop(0, nc, v_chunk, 0, unroll=4)

    nrow = seq // LANES

    def gather_rows(j, carry):
        rs = pl.ds(pl.multiple_of(j * LANES, LANES), LANES)
        gates = g_ref[rs, 0:N_GATE_COLS] + gb_ref[:, 0:N_GATE_COLS]
        y_in[pl.ds(pl.multiple_of(j * N_GATE_COLS, N_GATE_COLS), N_GATE_COLS), :] = gates.T
        return carry

    lax.fori_loop(0, nrow, gather_rows, 0, unroll=4)

    def gate_row(q):
        return y_in[pl.ds(q * MLSTM_HEADS + head, nrow, stride=N_GATE_COLS), :]

    lane_c = lax.broadcasted_iota(jnp.int32, (nrow, LANES), 1)
    row_c = lax.broadcasted_iota(jnp.int32, (nrow, LANES), 0)
    pos = lane_c & (L - 1)
    upper = lane_c >= L
    chunk_id = 2 * row_c + jnp.where(upper, 1, 0)

    def seg_scan(x, op, ident, forward):
        sh = 1
        while sh < L:
            if forward:
                x = op(x, jnp.where(pos >= sh, pltpu.roll(x, sh, 1), ident))
            else:
                x = op(x, jnp.where(pos < L - sh, pltpu.roll(x, LANES - sh, 1), ident))
            sh *= 2
        return x

    def seg_all(x, op):
        sh = 1
        while sh < L:
            x = op(x, jnp.where((pos & sh) == 0, pltpu.roll(x, LANES - sh, 1), pltpu.roll(x, sh, 1)))
            sh *= 2
        return x

    def chunk_shift(x, k, forward):
        if k == 1:
            r = pltpu.roll(x, L, 1)
            if forward:
                return jnp.where(upper, r, pltpu.roll(r, 1, 0))
            return jnp.where(upper, pltpu.roll(r, nrow - 1, 0), r)
        return pltpu.roll(x, k // 2, 0) if forward else pltpu.roll(x, nrow - k // 2, 0)

    def gate_direction(forward, base):
        i_pre = gate_row(base)
        log_f = _log_sigmoid(gate_row(base + 1))
        cum = seg_scan(log_f, jnp.add, 0.0, forward)
        b_src = i_pre - cum
        b_max = seg_scan(b_src, jnp.maximum, -jnp.inf, forward)
        edge = (L - 1) if forward else 0
        total = seg_all(jnp.where(pos == edge, cum, 0.0), jnp.add)
        src = total - cum + i_pre
        src_max = seg_all(src, jnp.maximum)
        acc_a, acc_b = total, src_max
        k = 1
        while k < nc:
            valid = (chunk_id >= k) if forward else (chunk_id <= nc - 1 - k)
            prev_a, prev_b = chunk_shift(acc_a, k, forward), chunk_shift(acc_b, k, forward)
            acc_b = jnp.where(valid, jnp.maximum(prev_b + acc_a, acc_b), acc_b)
            acc_a = jnp.where(valid, acc_a + prev_a, acc_a)
            k *= 2
        m_new = jnp.maximum(acc_a + NEG_BIG, acc_b)
        first = (chunk_id == 0) if forward else (chunk_id == nc - 1)
        m_prev = jnp.where(first, NEG_BIG, chunk_shift(m_new, 1, forward))
        m_rel = jnp.maximum(m_prev, b_max)
        o = 0 if forward else N_ROWS
        rows_scr[o + 0] = -m_rel
        rows_scr[o + 1] = jnp.exp(src - m_new)
        rows_scr[o + 2] = jnp.exp(m_prev - m_rel)
        rows_scr[o + 3] = jnp.exp(-(cum + m_rel))
        rows_scr[o + 4] = jnp.exp(total + m_prev - m_new)
        y_out[pl.ds(0 if forward else 1, nrow, stride=N_GATE_COLS), :] = b_src

    y_out[...] = jnp.zeros_like(y_out)
    gate_direction(True, 0)
    gate_direction(False, 2)

    def scatter_cols(j, carry):
        tile = y_out[pl.ds(pl.multiple_of(j * N_GATE_COLS, N_GATE_COLS), N_GATE_COLS), :]
        cols_scr[pl.ds(pl.multiple_of(j * LANES, LANES), LANES), 0:N_GATE_COLS] = tile.T
        return carry

    lax.fori_loop(0, nrow, scatter_cols, 0, unroll=4)

    s_idx = lax.broadcasted_iota(jnp.int32, (L, L), 0)
    t_idx = lax.broadcasted_iota(jnp.int32, (L, L), 1)

    def both(it, carry):
        states = list(carry)
        half = MLSTM_GROUP // 2
        tasks = []
        for g in range(MLSTM_GROUP):
            tasks.append((0, it * MLSTM_GROUP + g, it * half, g // 2, g % 2, s_idx <= t_idx, h_f))
        for g in range(MLSTM_GROUP):
            tasks.append((1, nc - 1 - (it * MLSTM_GROUP + g), nrow - (it + 1) * half,
                          half - 1 - g // 2, 1 - g % 2, s_idx >= t_idx, h_b))
        work = []
        for dirn, c, row0, r, hf, mask, h_ref in tasks:
            rs = pl.ds(pl.multiple_of(c * L, L), L)
            rows = [rows_scr[dirn * N_ROWS + i, pl.ds(row0, half), :][r:r + 1, hf * L:(hf + 1) * L]
                    for i in range(N_ROWS)]
            a_row, w_src, w_inter, e_neg, carry_scale = rows
            q, k, vta = qs[rs, :], ks[rs, :], vt[c]
            kq = _dot_nt(k, q)
            upd = jnp.dot((vta.astype(F32) * w_src).astype(BF16), k, preferred_element_type=F32)
            work.append(dict(dirn=dirn, rs=rs, q=q, vta=vta, kq=kq, upd=upd, mask=mask, h_ref=h_ref,
                             a_row=a_row, w_inter=w_inter, e_neg=e_neg, carry_scale=carry_scale[:, 0:1],
                             b_col=cols_scr[rs, dirn:dirn + 1]))
        for w in work:
            w["c_aug"] = states[w["dirn"]]
            states[w["dirn"]] = w["carry_scale"] * w["c_aug"] + w["upd"]
        for w in work:
            w["cq"] = _dot_nt(w["c_aug"].astype(BF16), w["q"])
        for w in work:
            decay = jnp.where(w["mask"], w["b_col"] + w["a_row"], -jnp.inf)
            w["st"] = (w["kq"] * jnp.exp(decay)).astype(BF16)
        for w in work:
            w["sv"] = jnp.dot(w["vta"], w["st"], preferred_element_type=F32)
        for w in work:
            num = w["sv"] + w["w_inter"] * w["cq"]
            den = num[MLSTM_HEAD_DIM:MLSTM_HEAD_DIM + 1, :]
            h_t = num[0:MLSTM_HEAD_DIM, :] * (1.0 / jnp.maximum(jnp.abs(den), w["e_neg"]))
            w["h_ref"][w["rs"], :] = h_t.T
        return tuple(states)

    init = jnp.zeros((MLSTM_AUG, MLSTM_HEAD_DIM), F32)
    lax.fori_loop(0, nc // MLSTM_GROUP, both, (init, init))

    def finish(c, carry):
        rs = pl.ds(pl.multiple_of(c * CONV_ROWS, CONV_ROWS), CONV_ROWS)
        cell = h_f[rs, :] + h_b[rs, :]
        o_ref[rs, :] = jax.nn.sigmoid(og_ref[rs, :]) * (_rms(cell) * gain_ref[...])
        return carry

    lax.fori_loop(0, seq // CONV_ROWS, finish, 0)


def _mlstm(proj, gate_bias_row, conv_w, conv_b, gain, batch, seq):
    blk = lambda col: pl.BlockSpec((None, seq, LANES), lambda b, h: (b, 0, col + h))
    row = lambda off: pl.BlockSpec((1, LANES), lambda b, h: (0, off + h))
    tap = lambda off: pl.BlockSpec((CONV_WIDTH, LANES), lambda b, h: (0, off + h))
    return pl.pallas_call(
        _mlstm_body,
        grid=(batch, MLSTM_HEADS),
        in_specs=[
            blk(COL_MQ), blk(COL_MK), blk(COL_MV), blk(COL_MO),
            pl.BlockSpec((None, seq, LANES), lambda b, h: (b, 0, COL_MG)),
            pl.BlockSpec((1, LANES), lambda b, h: (0, 0)),
            tap(0), tap(MLSTM_HEADS), row(0), row(MLSTM_HEADS), row(0),
        ],
        out_specs=pl.BlockSpec((None, seq, LANES), lambda b, h: (b, 0, h)),
        out_shape=jax.ShapeDtypeStruct((batch, seq, MLSTM_WIDTH), F32),
        scratch_shapes=(
            [pltpu.VMEM((seq + 2 * CONV_HALO, LANES), F32)]
            + [pltpu.VMEM((seq, LANES), BF16) for _ in range(2)]
            + [pltpu.VMEM((seq // MLSTM_CHUNK, MLSTM_AUG, MLSTM_CHUNK), BF16),
               pltpu.VMEM((seq // LANES * N_GATE_COLS, LANES), F32),
               pltpu.VMEM((seq // LANES * N_GATE_COLS, LANES), F32),
               pltpu.VMEM((2 * N_ROWS, seq // LANES, LANES), F32)]
            + [pltpu.VMEM((seq, LANES), F32) for _ in range(3)]),
        compiler_params=pltpu.CompilerParams(
            dimension_semantics=("parallel", "parallel"), vmem_limit_bytes=VMEM_LIMIT),
        name="mlstm",
    )(proj, proj, proj, proj, proj, gate_bias_row, conv_w, conv_w, conv_b, conv_b, gain)


def _rope_tables(seq):
    half = ROPE_DIM // 2
    inv_freq = ROPE_THETA ** (-jnp.arange(half, dtype=F32) / half)
    ang = jnp.arange(seq, dtype=F32)[:, None] * inv_freq[None, :]
    cos, sin = jnp.cos(ang), jnp.sin(ang)
    pad = jnp.zeros((seq, ATT_HEAD_DIM - ROPE_DIM), F32)
    zero = jnp.zeros((seq, half), F32)
    cos_t = jnp.concatenate([cos, cos, pad + 1.0], axis=1)
    sa_t = jnp.concatenate([-sin, zero, pad], axis=1)
    sb_t = jnp.concatenate([zero, sin, pad], axis=1)
    reps = LANES // ATT_HEAD_DIM
    return tuple(jnp.tile(t, (1, reps)) for t in (cos_t, sa_t, sb_t))


def kernel(x, norm_gain, ffn_w_in, ffn_w_out, mix_w_in, conv_w, conv_b, gate_bias, mlstm_norm_gain, mix_w_out):
    batch, seq, _ = x.shape
    depth = norm_gain.shape[0]
    w_in = ffn_w_in.astype(BF16)
    w_out = ffn_w_out.astype(BF16)
    mw_in = jnp.pad(mix_w_in, ((0, 0), (0, 0), (0, IN_PAD - mix_w_in.shape[2]))).astype(BF16)
    mw_out = mix_w_out.astype(BF16)
    gb_rows = jnp.pad(gate_bias.reshape(depth, 1, N_GATE_COLS), ((0, 0), (0, 0), (0, LANES - N_GATE_COLS)))
    cos, sa, sb = _rope_tables(seq)

    xt = x.reshape(batch * seq, D_MODEL)
    for layer in range(depth):
        g = norm_gain[layer][:, None, :]
        xt = _ffn(xt, g[0], g[1], w_in, w_out, layer, 0)
        proj = _mix_in(xt, g[2], mw_in[layer], cos, sa, sb, seq).reshape(batch, seq, IN_PAD)
        attn = _attn(proj, batch, seq)
        ml = _mlstm(proj, gb_rows[layer], conv_w[layer], conv_b[layer][None, :],
                    mlstm_norm_gain[layer][None, :], batch, seq)
        xt = _mix_ffn(xt, attn.reshape(batch * seq, ATT_WIDTH), ml.reshape(batch * seq, MLSTM_WIDTH),
                      mw_out.reshape(depth, 2, ATT_WIDTH, D_MODEL), g[3], g[4], g[5], w_in, w_out, layer)
    return xt.reshape(batch, seq, D_MODEL)
```
